```python
import jax
import jax.numpy as jnp
from jax import lax
import numpy as np

D_MODEL = 2048
BATCH = 4
SEQ = 8192
DEPTH = 1
DEC_BATCH = 32
DEC_SEQ = 16
PAST_LEN = 4096

CHUNK = 64
GLA_HEADS = 4
GLA_DK = D_MODEL // (2 * GLA_HEADS)
GLA_DV = D_MODEL // GLA_HEADS
GLA_RANK = 16
GLA_TAU = 16.0
SB_HEADS = 16
SB_DH = D_MODEL // SB_HEADS
SB_QBLOCK = 128
N_EXPERTS = 32
TOP_K = 4
D_FF = D_MODEL
SWIGLU_LIMIT = 7.0
SWIGLU_ALPHA = 1.702
ROUTE_BLOCK = 256
DN_ALPHA = (2.0 * DEPTH) ** 0.25
DN_BETA = (8.0 * DEPTH) ** -0.25
LN_EPS = 1e-5
RMS_EPS = 1e-6
GLA_QK = GLA_HEADS * GLA_DK
GLA_V = GLA_HEADS * GLA_DV
SB_W = SB_HEADS * SB_DH
IN_SPLITS = (GLA_QK, GLA_QK, GLA_V, GLA_V, GLA_RANK, SB_W, SB_W, SB_W, D_MODEL, D_MODEL)
IN_IS_VALUE = (False, False, True, False, False, False, False, True, False, False)
D_IN = sum(IN_SPLITS)

kernel_name = 'hybrid_gla_stickbreak_moe_stream_step'


def _layernorm(x, g, b):
    xf = x.astype(jnp.float32)
    mu = jnp.mean(xf, axis=-1, keepdims=True)
    var = jnp.mean(jnp.square(xf - mu), axis=-1, keepdims=True)
    return ((xf - mu) * lax.rsqrt(var + LN_EPS) * g + b).astype(x.dtype)


def _head_rmsnorm(o, g):
    of = o.astype(jnp.float32)
    return of * lax.rsqrt(jnp.mean(of * of, axis=-1, keepdims=True) + RMS_EPS) * g


def _split_in(proj):
    bounds = [int(b) for b in np.cumsum(IN_SPLITS)[:-1]]
    return jnp.split(proj, bounds, axis=-1)


def _gla(q, k, v, log_a, s0):
    b_, t, h, _ = q.shape
    c = min(CHUNK, t)
    n = t // c

    def to_chunks(a):
        return a.reshape(b_, n, c, h, a.shape[-1]).transpose(1, 0, 3, 2, 4)

    qc, kc, vc, ac = to_chunks(q), to_chunks(k), to_chunks(v), to_chunks(log_a)
    causal = jnp.tril(jnp.ones((c, c), bool))[:, :, None]

    def step(s, inp):
        qi, ki, vi, ai = inp
        cum = jnp.cumsum(ai.astype(jnp.float32), axis=2)
        o_inter = jnp.einsum('bhtd,bhde->bhte', qi * jnp.exp(cum), s)
        diff = cum[:, :, :, None, :] - cum[:, :, None, :, :]
        decay = jnp.exp(jnp.where(causal, diff, -jnp.inf))
        scores = jnp.einsum('bhtd,bhsd,bhtsd->bhts', qi, ki, decay)
        o_intra = jnp.einsum('bhts,bhse->bhte', scores, vi)
        last = cum[:, :, -1:, :]
        s_new = jnp.exp(last[:, :, 0, :])[..., None] * s + jnp.einsum(
            'bhsd,bhse->bhde', ki * jnp.exp(last - cum), vi)
        return s_new, o_inter + o_intra

    s_fin, o = lax.scan(step, s0.astype(jnp.float32), (qc, kc, vc, ac))
    o = o.transpose(1, 0, 3, 2, 4).reshape(b_, t, h, -1)
    return o, s_fin.astype(s0.dtype)


def _stick_breaking(q, k, v, q_offset):
    b_, tq, h, dh = q.shape
    tk = k.shape[1]
    blk = min(SB_QBLOCK, tq)
    nb = tq // blk
    qb = q.reshape(b_, nb, blk, h, dh).transpose(1, 0, 2, 3, 4)
    key_pos = jnp.arange(tk)
    scale = dh ** -0.5

    def block(args):
        qi, i = args
        q_pos = q_offset + i * blk + jnp.arange(blk)
        visible = (key_pos[None, :] < q_pos[:, None])[None, None]
        z = jnp.einsum('bqhd,bkhd->bhqk', qi, k).astype(jnp.float32) * scale
        log_beta = jax.nn.log_sigmoid(z)
        log_1m_beta = jnp.where(visible, jax.nn.log_sigmoid(-z), 0.0)
        suffix = lax.cumsum(log_1m_beta, axis=3, reverse=True)
        log_w = log_beta + (suffix - log_1m_beta)
        w = jnp.where(visible, jnp.exp(log_w), 0.0)
        return jnp.einsum('bhqk,bkhd->bqhd', w.astype(v.dtype), v)

    o = lax.map(block, (qb, jnp.arange(nb)))
    return o.transpose(1, 0, 2, 3, 4).reshape(b_, tq, h, dh)


def _token_mixer(h, gla_s0, k_past, v_past, q_offset, w_in, w_gla_f, b_gla_f, gla_norm_g,
                 w_gla_out, w_sb_out, w_o):
    b_, t, _ = h.shape
    gq, gk, gv, gg, gf, sq, sk, sv, gate_a, gate_b = _split_in(h @ w_in)

    def heads(a, n_heads):
        return a.reshape(b_, t, n_heads, -1)

    log_a = jax.nn.log_sigmoid((gf @ w_gla_f + b_gla_f).astype(jnp.float32)) / GLA_TAU
    o_gla, s_fin = _gla(heads(gq, GLA_HEADS) * GLA_DK ** -0.5, heads(gk, GLA_HEADS),
                        heads(gv, GLA_HEADS), heads(log_a, GLA_HEADS), gla_s0)
    o_gla = _head_rmsnorm(o_gla, gla_norm_g) * jax.nn.silu(heads(gg, GLA_HEADS).astype(jnp.float32))
    branch_a = o_gla.reshape(b_, t, GLA_V).astype(h.dtype) @ w_gla_out

    k_new, v_new = heads(sk, SB_HEADS), heads(sv, SB_HEADS)
    if k_past is None:
        k_all, v_all = k_new, v_new
    else:
        k_all = jnp.concatenate([k_past.astype(k_new.dtype), k_new], axis=1)
        v_all = jnp.concatenate([v_past.astype(v_new.dtype), v_new], axis=1)
    o_sb = _stick_breaking(heads(sq, SB_HEADS), k_all, v_all, q_offset)
    branch_b = o_sb.reshape(b_, t, SB_W) @ w_sb_out

    merged = jax.nn.sigmoid(gate_a) * branch_a + jax.nn.sigmoid(gate_b) * branch_b
    return merged @ w_o, s_fin, k_new, v_new


def _clamped_swiglu(gu):
    gate, up = gu[..., ::2], gu[..., 1::2]
    gate = jnp.minimum(gate, SWIGLU_LIMIT)
    up = jnp.clip(up, -SWIGLU_LIMIT, SWIGLU_LIMIT)
    return (up + 1.0) * (gate * jax.nn.sigmoid(SWIGLU_ALPHA * gate))


def _moe(h, w_router, b_router, w_gate_up, b_gate_up, w_down, b_down):
    n, d = h.shape
    logits = (h @ w_router + b_router).astype(jnp.float32)
    top_val, top_idx = lax.top_k(logits, TOP_K)
    top_w = jax.nn.softmax(top_val, axis=-1)
    n_assign = n * TOP_K
    flat_e = top_idx.reshape(-1)
    order = jnp.argsort(flat_e)
    e_sorted = flat_e[order]
    tok_sorted = (order // TOP_K).astype(jnp.int32)
    counts = jnp.zeros((N_EXPERTS,), jnp.int32).at[flat_e].add(1)
    padded = (counts + ROUTE_BLOCK - 1) // ROUTE_BLOCK * ROUTE_BLOCK
    pad_end = jnp.cumsum(padded)
    pad_start = pad_end - padded
    grp_start = jnp.cumsum(counts) - counts
    dest = pad_start[e_sorted] + jnp.arange(n_assign, dtype=jnp.int32) - grp_start[e_sorted]
    n_blocks = -(-(n_assign + N_EXPERTS * (ROUTE_BLOCK - 1)) // ROUTE_BLOCK)
    n_rows = n_blocks * ROUTE_BLOCK
    slot_tok = jnp.full((n_rows,), n, jnp.int32).at[dest].set(tok_sorted)
    slot_w = jnp.zeros((n_rows,), jnp.float32).at[dest].set(top_w.reshape(-1)[order])
    block_e = jnp.minimum(
        jnp.searchsorted(pad_end, jnp.arange(n_blocks, dtype=jnp.int32) * ROUTE_BLOCK, side='right'),
        N_EXPERTS - 1)
    h_pad = jnp.concatenate([h, jnp.zeros((1, d), h.dtype)], axis=0)

    def expert_block(args):
        rows, e = args
        gu = h_pad[rows] @ w_gate_up[e] + b_gate_up[e]
        return _clamped_swiglu(gu) @ w_down[e] + b_down[e]

    yb = lax.map(expert_block, (slot_tok.reshape(n_blocks, ROUTE_BLOCK), block_e))
    yb = yb.reshape(n_rows, d) * slot_w[:, None]
    out = jax.ops.segment_sum(yb, slot_tok, num_segments=n + 1)
    return out[:n].astype(h.dtype)


def _layer(x, c, gla_s0, k_past, v_past, q_offset, p):
    (w_ada, b_ada, w_in, w_gla_f, b_gla_f, gla_norm_g, w_gla_out, w_sb_out, w_o,
     ln1_g, ln1_b, w_router, b_router, w_gate_up, b_gate_up, w_down, b_down, ln2_g, ln2_b) = p
    mod = (jax.nn.silu(c) @ w_ada + b_ada)[:, None, :]
    sh1, sc1, g1, sh2, sc2, g2 = jnp.split(mod, 6, axis=-1)
    h = x * (1.0 + sc1) + sh1
    mix, s_fin, k_new, v_new = _token_mixer(h, gla_s0, k_past, v_past, q_offset, w_in, w_gla_f,
                                            b_gla_f, gla_norm_g, w_gla_out, w_sb_out, w_o)
    x = _layernorm(DN_ALPHA * x + g1 * mix, ln1_g, ln1_b)
    h = x * (1.0 + sc2) + sh2
    b_, t, d = h.shape
    ffn = _moe(h.reshape(b_ * t, d), w_router, b_router, w_gate_up, b_gate_up,
               w_down, b_down).reshape(b_, t, d)
    x = _layernorm(DN_ALPHA * x + g2 * ffn, ln2_g, ln2_b)
    return x, s_fin, k_new, v_new


def setup_inputs(seed: int = 0) -> dict:
    key = jax.random.key(seed)
    ks = jax.random.split(key, 26)
    f32 = jnp.float32
    L = DEPTH

    def nrm(k, shape, scale=1.0):
        return jax.random.normal(k, shape, f32) * scale

    col_scale = jnp.concatenate([jnp.full((s,), DN_BETA if is_v else 1.0, f32)
                                 for s, is_v in zip(IN_SPLITS, IN_IS_VALUE)])
    return {
        'x_prompt': nrm(ks[0], (BATCH, SEQ, D_MODEL)),
        'x_sample': nrm(ks[1], (DEC_BATCH, DEC_SEQ, D_MODEL)),
        'c_prompt': nrm(ks[2], (BATCH, D_MODEL)),
        'c_sample': nrm(ks[3], (DEC_BATCH, D_MODEL)),
        'state_gla': nrm(ks[4], (L, DEC_BATCH, GLA_HEADS, GLA_DK, GLA_DV)),
        'cache_sb_k': nrm(ks[5], (L, DEC_BATCH, PAST_LEN, SB_HEADS, SB_DH)),
        'cache_sb_v': nrm(ks[6], (L, DEC_BATCH, PAST_LEN, SB_HEADS, SB_DH), DN_BETA),
        'w_ada': nrm(ks[7], (L, D_MODEL, 6 * D_MODEL), 0.5 * D_MODEL ** -0.5),
        'b_ada': nrm(ks[8], (L, 6 * D_MODEL), 0.02),
        'w_in': nrm(ks[9], (L, D_MODEL, D_IN), D_MODEL ** -0.5) * col_scale,
        'w_gla_f': nrm(ks[10], (L, GLA_RANK, GLA_QK), GLA_RANK ** -0.5),
        'b_gla_f': nrm(ks[11], (L, GLA_QK), 0.1),
        'gla_norm_g': 1.0 + nrm(ks[12], (L, GLA_DV), 0.05),
        'w_gla_out': nrm(ks[13], (L, GLA_V, D_MODEL), DN_BETA * GLA_V ** -0.5),
        'w_sb_out': nrm(ks[14], (L, SB_W, D_MODEL), DN_BETA * SB_W ** -0.5),
        'w_o': nrm(ks[15], (L, D_MODEL, D_MODEL), DN_BETA * D_MODEL ** -0.5),
        'ln1_g': 1.0 + nrm(ks[16], (L, D_MODEL), 0.05),
        'ln1_b': nrm(ks[17], (L, D_MODEL), 0.02),
        'w_router': nrm(ks[18], (L, D_MODEL, N_EXPERTS), D_MODEL ** -0.5),
        'b_router': nrm(ks[19], (L, N_EXPERTS), 0.01),
        'w_gate_up': nrm(ks[20], (L, N_EXPERTS, D_MODEL, 2 * D_FF), D_MODEL ** -0.5),
        'b_gate_up': nrm(ks[21], (L, N_EXPERTS, 2 * D_FF), 0.02),
        'w_down': nrm(ks[22], (L, N_EXPERTS, D_FF, D_MODEL), DN_BETA * D_FF ** -0.5),
        'b_down': nrm(ks[23], (L, N_EXPERTS, D_MODEL), 0.02),
        'ln2_g': 1.0 + nrm(ks[24], (L, D_MODEL), 0.05),
        'ln2_b': nrm(ks[25], (L, D_MODEL), 0.02),
    }


def reference(x_prompt, x_sample, c_prompt, c_sample, state_gla, cache_sb_k, cache_sb_v,
              w_ada, b_ada, w_in, w_gla_f, b_gla_f, gla_norm_g, w_gla_out, w_sb_out, w_o,
              ln1_g, ln1_b, w_router, b_router, w_gate_up, b_gate_up, w_down, b_down,
              ln2_g, ln2_b):
    y_p, y_s = x_prompt, x_sample
    past_len = cache_sb_k.shape[2]
    gla_p, k_p, v_p, gla_s, k_s, v_s = [], [], [], [], [], []
    for l in range(DEPTH):
        p = (w_ada[l], b_ada[l], w_in[l], w_gla_f[l], b_gla_f[l], gla_norm_g[l], w_gla_out[l],
             w_sb_out[l], w_o[l], ln1_g[l], ln1_b[l], w_router[l], b_router[l], w_gate_up[l],
             b_gate_up[l], w_down[l], b_down[l], ln2_g[l], ln2_b[l])
        s0 = jnp.zeros((x_prompt.shape[0], GLA_HEADS, GLA_DK, GLA_DV), x_prompt.dtype)
        y_p, sf, kn, vn = _layer(y_p, c_prompt, s0, None, None, 0, p)
        gla_p.append(sf)
        k_p.append(kn)
        v_p.append(vn)
        y_s, sf, kn, vn = _layer(y_s, c_sample, state_gla[l], cache_sb_k[l], cache_sb_v[l], past_len, p)
        gla_s.append(sf)
        k_s.append(kn)
        v_s.append(vn)
    return (y_p, y_s, jnp.stack(gla_p), jnp.stack(k_p), jnp.stack(v_p),
            jnp.stack(gla_s), jnp.stack(k_s), jnp.stack(v_s))
```

```python
import functools

import jax
import jax.numpy as jnp
from jax import lax
from jax.experimental import pallas as pl
from jax.experimental.pallas import tpu as pltpu

F32 = jnp.float32
BF16 = jnp.bfloat16

GLA_CHUNK = 64
GLA_SUB = 16
GLA_TAU = 16.0
TOP_K = 4
SWIGLU_LIMIT = 7.0
SWIGLU_ALPHA = 1.702
LN_EPS = 1e-5
RMS_EPS = 1e-6
LANES = 128
VMEM_LIMIT_BYTES = 56 * 1024 * 1024


def _cparams(*sem):
    return pltpu.CompilerParams(dimension_semantics=sem, vmem_limit_bytes=VMEM_LIMIT_BYTES)


def _log_sigmoid(x):
    return jnp.minimum(x, 0.0) - jnp.log1p(jnp.exp(-jnp.abs(x)))


def _layernorm(u, g, b):
    mu = jnp.mean(u, axis=-1, keepdims=True)
    d = u - mu
    var = jnp.mean(d * d, axis=-1, keepdims=True)
    return d * lax.rsqrt(var + LN_EPS) * g + b


def _mod_operand(mod, n_tok, t_per_batch, tm):
    b, d = mod.shape
    if t_per_batch % tm == 0:
        per = t_per_batch // tm
        return mod.reshape(b, 1, d), pl.BlockSpec((None, 1, d), lambda i, *_: (i // per, 0, 0))
    rows = jnp.repeat(mod, t_per_batch, axis=0)
    return rows, pl.BlockSpec((tm, d), lambda i, *_: (i, 0))


def _ada_kernel(c_ref, w_ref, b_ref, o_ref):
    c = c_ref[...]
    s = c * jax.nn.sigmoid(c)
    o_ref[...] = jnp.dot(s.astype(BF16), w_ref[...].astype(BF16),
                         preferred_element_type=F32) + b_ref[...]


def _ada(c, w, b):
    bc, d = c.shape
    n = w.shape[1]
    tn = min(1024, n)
    return pl.pallas_call(
        _ada_kernel,
        grid=(n // tn,),
        in_specs=[pl.BlockSpec((bc, d), lambda j: (0, 0)),
                  pl.BlockSpec((d, tn), lambda j: (0, j)),
                  pl.BlockSpec((1, tn), lambda j: (0, j))],
        out_specs=pl.BlockSpec((bc, tn), lambda j: (0, j)),
        out_shape=jax.ShapeDtypeStruct((bc, n), F32),
        compiler_params=_cparams("parallel"),
        name="ada",
    )(c, w, b.reshape(1, n))


def _modmm_kernel(x_ref, sc_ref, sh_ref, w_ref, o_ref, hb_ref):
    @pl.when(pl.program_id(1) == 0)
    def _():
        hb_ref[...] = (x_ref[...] * (1.0 + sc_ref[...]) + sh_ref[...]).astype(BF16)

    o_ref[...] = jnp.dot(hb_ref[...], w_ref[...], preferred_element_type=F32).astype(o_ref.dtype)


def _modmm(x2, sc, sh, w, t_per_batch, out_dtype=F32, name="modmm"):
    n, d = x2.shape
    c = w.shape[1]
    tm = min(512, n)
    tn = min(512, c)
    sc_op, sc_spec = _mod_operand(sc, n, t_per_batch, tm)
    sh_op, sh_spec = _mod_operand(sh, n, t_per_batch, tm)
    return pl.pallas_call(
        _modmm_kernel,
        grid=(n // tm, c // tn),
        in_specs=[pl.BlockSpec((tm, d), lambda i, j: (i, 0)), sc_spec, sh_spec,
                  pl.BlockSpec((d, tn), lambda i, j: (0, j))],
        out_specs=pl.BlockSpec((tm, tn), lambda i, j: (i, j)),
        out_shape=jax.ShapeDtypeStruct((n, c), out_dtype),
        scratch_shapes=[pltpu.VMEM((tm, d), BF16)],
        compiler_params=_cparams("parallel", "arbitrary"),
        name=name,
    )(x2, sc_op, sh_op, w)


def _loga_kernel(x_ref, sc_ref, sh_ref, wgf_ref, wf_ref, bf_ref, o_ref):
    h = (x_ref[...] * (1.0 + sc_ref[...]) + sh_ref[...]).astype(BF16)
    gf = jnp.dot(h, wgf_ref[...], preferred_element_type=F32)
    z = jnp.dot(gf.astype(BF16), wf_ref[...], preferred_element_type=F32) + bf_ref[...]
    o_ref[...] = _log_sigmoid(z) / GLA_TAU


def _loga(x2, sc, sh, w_gf, w_f, b_f, t_per_batch):
    n, d = x2.shape
    rp = w_gf.shape[1]
    qk = w_f.shape[1]
    tm = min(512, n)
    sc_op, sc_spec = _mod_operand(sc, n, t_per_batch, tm)
    sh_op, sh_spec = _mod_operand(sh, n, t_per_batch, tm)
    return pl.pallas_call(
        _loga_kernel,
        grid=(n // tm,),
        in_specs=[pl.BlockSpec((tm, d), lambda i: (i, 0)), sc_spec, sh_spec,
                  pl.BlockSpec((d, rp), lambda i: (0, 0)),
                  pl.BlockSpec((rp, qk), lambda i: (0, 0)),
                  pl.BlockSpec((1, qk), lambda i: (0, 0))],
        out_specs=pl.BlockSpec((tm, qk), lambda i: (i, 0)),
        out_shape=jax.ShapeDtypeStruct((n, qk), F32),
        compiler_params=_cparams("parallel"),
        name="loga",
    )(x2, sc_op, sh_op, w_gf, w_f, b_f.reshape(1, qk))


def _gla_kernel(*refs, chunk, n_chunks, has_s0, q_scale):
    if has_s0:
        q_ref, k_ref, v_ref, gg_ref, la_ref, gn_ref, s0_ref, y_ref, sfin_ref, st_ref = refs
    else:
        q_ref, k_ref, v_ref, gg_ref, la_ref, gn_ref, y_ref, sfin_ref, st_ref = refs
    c_idx = pl.program_id(2)
    dk = q_ref.shape[-1]
    n_sub = chunk // GLA_SUB

    @pl.when(c_idx == 0)
    def _():
        if has_s0:
            st_ref[...] = s0_ref[...].T
        else:
            st_ref[...] = jnp.zeros_like(st_ref)

    row_c = lax.broadcasted_iota(jnp.int32, (chunk, chunk), 0)
    col_c = lax.broadcasted_iota(jnp.int32, (chunk, chunk), 1)
    tri_incl = (row_c >= col_c).astype(F32)
    row_s = lax.broadcasted_iota(jnp.int32, (GLA_SUB, GLA_SUB), 0)
    col_s = lax.broadcasted_iota(jnp.int32, (GLA_SUB, GLA_SUB), 1)
    gnorm = gn_ref[...]

    def chunk_body(ci, carry):
        r0 = pl.multiple_of(ci * chunk, chunk)
        q = q_ref[pl.ds(r0, chunk), :] * q_scale
        k = k_ref[pl.ds(r0, chunk), :]
        v = v_ref[pl.ds(r0, chunk), :]
        la = la_ref[pl.ds(r0, chunk), :]
        vb = v.astype(BF16)
        cum = jnp.dot(tri_incl, la, preferred_element_type=F32, precision=lax.Precision.HIGHEST)
        st = st_ref[...]
        o = lax.dot_general((q * jnp.exp(cum)).astype(BF16), st.astype(BF16),
                            (((1,), (1,)), ((), ())), preferred_element_type=F32)
        o_rows = []
        for i in range(n_sub):
            lo = i * GLA_SUB
            qi, ki, ci_ = q[lo:lo + GLA_SUB], k[lo:lo + GLA_SUB], cum[lo:lo + GLA_SUB]
            sd = jnp.zeros((GLA_SUB, GLA_SUB), F32)
            for j in range(GLA_SUB):
                e = jnp.exp(jnp.minimum(ci_ - ci_[j:j + 1], 0.0))
                col = jnp.sum(qi * ki[j:j + 1] * e, axis=1, keepdims=True)
                sd = jnp.where(col_s == j, col, sd)
            sd = jnp.where(row_s >= col_s, sd, 0.0)
            oi = jnp.dot(sd.astype(BF16), vb[lo:lo + GLA_SUB], preferred_element_type=F32)
            if i > 0:
                ref_row = cum[lo - 1:lo]
                qd = (qi * jnp.exp(ci_ - ref_row)).astype(BF16)
                kd = (k[:lo] * jnp.exp(ref_row - cum[:lo])).astype(BF16)
                so = lax.dot_general(qd, kd, (((1,), (1,)), ((), ())), preferred_element_type=F32)
                oi = oi + jnp.dot(so.astype(BF16), vb[:lo], preferred_element_type=F32)
            o_rows.append(oi)
        o = o + (jnp.concatenate(o_rows, axis=0) if n_sub > 1 else o_rows[0])
        last = cum[chunk - 1:chunk]
        kdec = (k * jnp.exp(last - cum)).astype(BF16)
        st_ref[...] = st * jnp.exp(last) + lax.dot_general(
            vb, kdec, (((0,), (0,)), ((), ())), preferred_element_type=F32)
        gg = gg_ref[pl.ds(r0, chunk), :]
        rms = lax.rsqrt(jnp.mean(o * o, axis=-1, keepdims=True) + RMS_EPS)
        y_ref[pl.ds(r0, chunk), :] = (o * rms * gnorm * (gg * jax.nn.sigmoid(gg))).astype(y_ref.dtype)
        return carry

    lax.fori_loop(0, n_chunks, chunk_body, 0)

    @pl.when(c_idx == pl.num_programs(2) - 1)
    def _():
        sfin_ref[...] = st_ref[...].T


def _gla(g4, la, gnorm, s0, n_heads, dk, dv):
    b, t, _ = g4.shape
    chunk = min(GLA_CHUNK, t)
    tc = min(512, t)
    n_chunks = tc // chunk
    has_s0 = s0 is not None
    kq = (n_heads * dk) // dk
    kv = (2 * n_heads * dk) // dv
    kg = kv + n_heads
    in_specs = [pl.BlockSpec((None, tc, dk), lambda bi, h, c: (bi, c, h)),
                pl.BlockSpec((None, tc, dk), lambda bi, h, c: (bi, c, kq + h)),
                pl.BlockSpec((None, tc, dv), lambda bi, h, c: (bi, c, kv + h)),
                pl.BlockSpec((None, tc, dv), lambda bi, h, c: (bi, c, kg + h)),
                pl.BlockSpec((None, tc, dk), lambda bi, h, c: (bi, c, h)),
                pl.BlockSpec((1, dv), lambda bi, h, c: (0, 0))]
    args = [g4, g4, g4, g4, la, gnorm.reshape(1, dv)]
    if has_s0:
        in_specs.append(pl.BlockSpec((None, None, dk, dv), lambda bi, h, c: (bi, h, 0, 0)))
        args.append(s0)
    kern = functools.partial(_gla_kernel, chunk=chunk, n_chunks=n_chunks, has_s0=has_s0,
                             q_scale=dk ** -0.5)
    return pl.pallas_call(
        kern,
        grid=(b, n_heads, t // tc),
        in_specs=in_specs,
        out_specs=[pl.BlockSpec((None, tc, dv), lambda bi, h, c: (bi, c, h)),
                   pl.BlockSpec((None, None, dk, dv), lambda bi, h, c: (bi, h, 0, 0))],
        out_shape=[jax.ShapeDtypeStruct((b, t, n_heads * dv), BF16),
                   jax.ShapeDtypeStruct((b, n_heads, dk, dv), F32)],
        scratch_shapes=[pltpu.VMEM((dv, dk), F32)],
        compiler_params=_cparams("parallel", "parallel", "arbitrary"),
        name="gla",
    )(*args)


def _sb_tile(qb, k, v, run, acc, u, scale, masked):
    tq, tk = qb.shape[0], k.shape[0]
    z = lax.dot_general(qb, k.astype(BF16), (((1,), (1,)), ((), ())),
                        preferred_element_type=F32) * scale
    sp = jnp.log1p(jnp.exp(-jnp.abs(z)))
    log_beta = jnp.minimum(z, 0.0) - sp
    log_1m = -jnp.maximum(z, 0.0) - sp
    if masked:
        vis = (lax.broadcasted_iota(jnp.int32, (tq, tk), 1)
               < lax.broadcasted_iota(jnp.int32, (tq, tk), 0))
        log_1m = jnp.where(vis, log_1m, 0.0)
    hi = log_1m.astype(BF16)
    lo = (log_1m - hi.astype(F32)).astype(BF16)
    later = (jnp.dot(hi, u, preferred_element_type=F32)
             + jnp.dot(lo, u, preferred_element_type=F32))
    w = jnp.exp(log_beta + (later + run))
    if masked:
        w = jnp.where(vis, w, 0.0)
    acc = acc + jnp.dot(w.astype(BF16), v.astype(BF16), preferred_element_type=F32)
    run = run + (later[:, :1] + log_1m[:, :1])
    return run, acc


def _sb_kernel(*refs, tq, tkp, n_past, scale):
    if n_past:
        q_ref, kn_ref, vn_ref, kp_ref, vp_ref, u_ref, o_ref = refs
    else:
        q_ref, kn_ref, vn_ref, u_ref, o_ref = refs
    i = pl.program_id(2)
    dh = q_ref.shape[-1]
    qb = q_ref[...].astype(BF16)
    u_new = u_ref[:tq, :tq]
    run = jnp.zeros((tq, 1), F32)
    acc = jnp.zeros((tq, dh), F32)
    r0 = pl.multiple_of(i * tq, tq)
    run, acc = _sb_tile(qb, kn_ref[pl.ds(r0, tq), :], vn_ref[pl.ds(r0, tq), :], run, acc,
                        u_new, scale, True)

    def new_body(n, carry):
        r = pl.multiple_of((i - 1 - n) * tq, tq)
        return _sb_tile(qb, kn_ref[pl.ds(r, tq), :], vn_ref[pl.ds(r, tq), :], carry[0], carry[1],
                        u_new, scale, False)

    run, acc = lax.fori_loop(0, i, new_body, (run, acc))
    if n_past:
        u_past = u_ref[:tkp, :tkp]

        def past_body(n, carry):
            r = pl.multiple_of((n_past - 1 - n) * tkp, tkp)
            return _sb_tile(qb, kp_ref[pl.ds(r, tkp), :], vp_ref[pl.ds(r, tkp), :], carry[0],
                            carry[1], u_past, scale, False)

        run, acc = lax.fori_loop(0, n_past, past_body, (run, acc))
    o_ref[...] = acc.astype(o_ref.dtype)


def _stick_breaking(q, k_new, v_new, k_past, v_past, n_heads, dh):
    b, t, _ = q.shape
    tq = min(256, t)
    has_past = k_past is not None
    tp = k_past.shape[1] if has_past else 0
    tkp = min(256, tp) if has_past else 0
    n_past = tp // tkp if has_past else 0
    tu = max(tq, tkp)
    u = (lax.broadcasted_iota(jnp.int32, (tu, tu), 0)
         > lax.broadcasted_iota(jnp.int32, (tu, tu), 1)).astype(BF16)
    in_specs = [pl.BlockSpec((None, tq, dh), lambda bi, h, i: (bi, i, h)),
                pl.BlockSpec((None, t, dh), lambda bi, h, i: (bi, 0, h)),
                pl.BlockSpec((None, t, dh), lambda bi, h, i: (bi, 0, h))]
    args = [q, k_new, v_new]
    if has_past:
        in_specs += [pl.BlockSpec((None, tp, dh), lambda bi, h, i: (bi, 0, h)),
                     pl.BlockSpec((None, tp, dh), lambda bi, h, i: (bi, 0, h))]
        args += [k_past, v_past]
    in_specs.append(pl.BlockSpec((tu, tu), lambda bi, h, i: (0, 0)))
    args.append(u)
    kern = functools.partial(_sb_kernel, tq=tq, tkp=tkp, n_past=n_past, scale=dh ** -0.5)
    return pl.pallas_call(
        kern,
        grid=(b, n_heads, t // tq),
        in_specs=in_specs,
        out_specs=pl.BlockSpec((None, tq, dh), lambda bi, h, i: (bi, i, h)),
        out_shape=jax.ShapeDtypeStruct((b, t, n_heads * dh), BF16),
        compiler_params=_cparams("parallel", "parallel", "arbitrary"),
        name="stick_breaking",
    )(*args)


def _merge_kernel(ya_ref, yb_ref, wa_ref, wb_ref, ga_ref, gb_ref, o_ref):
    a = jnp.dot(ya_ref[...], wa_ref[...], preferred_element_type=F32)
    bb = jnp.dot(yb_ref[...], wb_ref[...], preferred_element_type=F32)
    o_ref[...] = (jax.nn.sigmoid(ga_ref[...]) * a + jax.nn.sigmoid(gb_ref[...]) * bb).astype(o_ref.dtype)


def _merge(ya, yb, wa, wb, gates):
    n, da = ya.shape
    db = yb.shape[1]
    d = wa.shape[1]
    tm = min(512, n)
    tn = min(512, d)
    nj = d // tn
    return pl.pallas_call(
        _merge_kernel,
        grid=(n // tm, nj),
        in_specs=[pl.BlockSpec((tm, da), lambda i, j: (i, 0)),
                  pl.BlockSpec((tm, db), lambda i, j: (i, 0)),
                  pl.BlockSpec((da, tn), lambda i, j: (0, j)),
                  pl.BlockSpec((db, tn), lambda i, j: (0, j)),
                  pl.BlockSpec((tm, tn), lambda i, j: (i, j)),
                  pl.BlockSpec((tm, tn), lambda i, j: (i, nj + j))],
        out_specs=pl.BlockSpec((tm, tn), lambda i, j: (i, j)),
        out_shape=jax.ShapeDtypeStruct((n, d), BF16),
        compiler_params=_cparams("parallel", "arbitrary"),
        name="merge",
    )(ya, yb, wa, wb, gates, gates)


def _post_kernel(m_ref, wo_ref, x_ref, g1_ref, sc_ref, sh_ref, lg_ref, lb_ref, wr_ref, br_ref,
                 x1_ref, h2_ref, lo_ref, *, alpha):
    mix = jnp.dot(m_ref[...], wo_ref[...], preferred_element_type=F32)
    x1 = _layernorm(alpha * x_ref[...] + g1_ref[...] * mix, lg_ref[...], lb_ref[...])
    h2 = x1 * (1.0 + sc_ref[...]) + sh_ref[...]
    x1_ref[...] = x1
    h2_ref[...] = h2
    lo_ref[...] = jnp.dot(h2.astype(BF16), wr_ref[...], preferred_element_type=F32) + br_ref[...]


def _post(merged, wo, x2, g1, sc2, sh2, ln_g, ln_b, wr, br, t_per_batch, alpha):
    n, d = x2.shape
    e = wr.shape[1]
    tm = min(256, n)
    g1_op, g1_spec = _mod_operand(g1, n, t_per_batch, tm)
    sc_op, sc_spec = _mod_operand(sc2, n, t_per_batch, tm)
    sh_op, sh_spec = _mod_operand(sh2, n, t_per_batch, tm)
    row = lambda i: (i, 0)
    fix = lambda i: (0, 0)
    return pl.pallas_call(
        functools.partial(_post_kernel, alpha=alpha),
        grid=(n // tm,),
        in_specs=[pl.BlockSpec((tm, d), row), pl.BlockSpec((d, d), fix), pl.BlockSpec((tm, d), row),
                  g1_spec, sc_spec, sh_spec,
                  pl.BlockSpec((1, d), fix), pl.BlockSpec((1, d), fix),
                  pl.BlockSpec((d, e), fix), pl.BlockSpec((1, e), fix)],
        out_specs=[pl.BlockSpec((tm, d), row), pl.BlockSpec((tm, d), row), pl.BlockSpec((tm, e), row)],
        out_shape=[jax.ShapeDtypeStruct((n, d), F32), jax.ShapeDtypeStruct((n, d), F32),
                   jax.ShapeDtypeStruct((n, e), F32)],
        compiler_params=_cparams("parallel"),
        name="post",
    )(merged, wo, x2, g1_op, sc_op, sh_op, ln_g.reshape(1, d), ln_b.reshape(1, d), wr,
      br.reshape(1, e))


def _cols_to_lanes(cols):
    tm = cols[0].shape[0]
    lane = lax.broadcasted_iota(jnp.int32, (tm, len(cols)), 1)
    out = jnp.broadcast_to(cols[0], (tm, len(cols)))
    for kk in range(1, len(cols)):
        out = jnp.where(lane == kk, cols[kk], out)
    return out


def _route_kernel(lo_ref, idx_ref, w_ref, rank_ref, cnt_ref, carry_ref):
    @pl.when(pl.program_id(0) == 0)
    def _():
        carry_ref[...] = jnp.zeros_like(carry_ref)

    lg = lo_ref[...]
    tm, e = lg.shape
    lane = lax.broadcasted_iota(jnp.int32, (tm, e), 1).astype(F32)
    vals, idxs = [], []
    chosen = jnp.zeros((tm, e), F32)
    for _ in range(TOP_K):
        m = jnp.max(lg, axis=-1, keepdims=True)
        ik = jnp.min(jnp.where(lg == m, lane, float(e)), axis=-1, keepdims=True)
        sel = lane == ik
        vals.append(m)
        idxs.append(ik)
        chosen = jnp.where(sel, 1.0, chosen)
        lg = jnp.where(sel, -jnp.inf, lg)
    ex = [jnp.exp(v - vals[0]) for v in vals]
    tot = ex[0]
    for t_ in ex[1:]:
        tot = tot + t_
    before = (lax.broadcasted_iota(jnp.int32, (tm, tm), 1)
              < lax.broadcasted_iota(jnp.int32, (tm, tm), 0)).astype(BF16)
    pos = jnp.dot(before, chosen.astype(BF16), preferred_element_type=F32) + carry_ref[...]
    ranks = [jnp.sum(jnp.where(lane == ik, pos, 0.0), axis=-1, keepdims=True) for ik in idxs]
    idx_ref[...] = _cols_to_lanes(idxs).astype(jnp.int32)
    w_ref[...] = _cols_to_lanes([x / tot for x in ex])
    rank_ref[...] = _cols_to_lanes(ranks).astype(jnp.int32)
    carry_ref[...] = carry_ref[...] + jnp.sum(chosen, axis=0, keepdims=True)
    cnt_ref[...] = carry_ref[...].astype(jnp.int32)


def _route(logits):
    n, e = logits.shape
    tm = min(256, n)
    row = lambda i: (i, 0)
    return pl.pallas_call(
        _route_kernel,
        grid=(n // tm,),
        in_specs=[pl.BlockSpec((tm, e), row)],
        out_specs=[pl.BlockSpec((tm, TOP_K), row), pl.BlockSpec((tm, TOP_K), row),
                   pl.BlockSpec((tm, TOP_K), row), pl.BlockSpec((1, e), lambda i: (0, 0))],
        out_shape=[jax.ShapeDtypeStruct((n, TOP_K), jnp.int32), jax.ShapeDtypeStruct((n, TOP_K), F32),
                   jax.ShapeDtypeStruct((n, TOP_K), jnp.int32), jax.ShapeDtypeStruct((1, e), jnp.int32)],
        scratch_shapes=[pltpu.VMEM((1, e), F32)],
        compiler_params=_cparams("arbitrary"),
        name="route",
    )(logits)


def _dest_kernel(idx_ref, rank_ref, start_ref, o_ref):
    idx = idx_ref[...]
    tm = idx.shape[0]
    e = start_ref.shape[1]
    lane = lax.broadcasted_iota(jnp.int32, (tm, e), 1)
    start = start_ref[...].astype(F32)
    cols = [jnp.sum(jnp.where(lane == idx[:, kk:kk + 1], start, 0.0), axis=-1, keepdims=True)
            for kk in range(TOP_K)]
    o_ref[...] = _cols_to_lanes(cols).astype(jnp.int32) + rank_ref[...]


def _dest(idx, rank, group_start):
    n = idx.shape[0]
    e = group_start.shape[1]
    tm = min(256, n)
    row = lambda i: (i, 0)
    return pl.pallas_call(
        _dest_kernel,
        grid=(n // tm,),
        in_specs=[pl.BlockSpec((tm, TOP_K), row), pl.BlockSpec((tm, TOP_K), row),
                  pl.BlockSpec((1, e), lambda i: (0, 0))],
        out_specs=pl.BlockSpec((tm, TOP_K), row),
        out_shape=jax.ShapeDtypeStruct((n, TOP_K), jnp.int32),
        compiler_params=_cparams("parallel"),
        name="dest",
    )(idx, rank, group_start)


def _row_copy(src_ref, src_row, dst_ref, dst_row, sem):
    return pltpu.make_async_copy(src_ref.at[pl.ds(src_row, 1), :], dst_ref.at[pl.ds(dst_row, 1), :], sem)


def _dispatch_kernel(dest_ref, h_ref, xs_in_ref, xs_ref, sem):
    del xs_in_ref
    tm = h_ref.shape[0]

    def issue(t, c):
        for kk in range(TOP_K):
            _row_copy(h_ref, t, xs_ref, dest_ref[t * TOP_K + kk], sem).start()
        return c

    lax.fori_loop(0, tm, issue, 0)

    def drain(t, c):
        for kk in range(TOP_K):
            _row_copy(h_ref, 0, xs_ref, 0, sem).wait()
        return c

    lax.fori_loop(0, tm, drain, 0)


def _dispatch(dest_flat, h2, n_rows):
    n, d = h2.shape
    tm = min(256, n)
    xs0 = jnp.zeros((n_rows, d), F32)
    return pl.pallas_call(
        _dispatch_kernel,
        grid=(n // tm,),
        in_specs=[pl.BlockSpec((tm * TOP_K,), lambda i: (i,), memory_space=pltpu.SMEM),
                  pl.BlockSpec((tm, d), lambda i: (i, 0)),
                  pl.BlockSpec(memory_space=pl.ANY)],
        out_specs=pl.BlockSpec(memory_space=pl.ANY),
        out_shape=jax.ShapeDtypeStruct((n_rows, d), F32),
        scratch_shapes=[pltpu.SemaphoreType.DMA(())],
        input_output_aliases={2: 0},
        compiler_params=_cparams("arbitrary"),
        name="dispatch",
    )(dest_flat, h2, xs0)


def _experts_kernel(be_ref, nu_ref, xs_ref, wg_ref, wu_ref, bg_ref, bu_ref, wd_ref, bd_ref,
                    y_ref, xb_ref):
    del be_ref
    j = pl.program_id(0)
    f = pl.program_id(1)

    @pl.when(j < nu_ref[0])
    def _():
        @pl.when(f == 0)
        def _():
            xb_ref[...] = xs_ref[...].astype(BF16)

        xb = xb_ref[...]
        gate = jnp.dot(xb, wg_ref[...], preferred_element_type=F32) + bg_ref[...]
        up = jnp.dot(xb, wu_ref[...], preferred_element_type=F32) + bu_ref[...]
        gate = jnp.minimum(gate, SWIGLU_LIMIT)
        up = jnp.clip(up, -SWIGLU_LIMIT, SWIGLU_LIMIT)
        act = (up + 1.0) * (gate * jax.nn.sigmoid(SWIGLU_ALPHA * gate))
        part = jnp.dot(act.astype(BF16), wd_ref[...], preferred_element_type=F32)

        @pl.when(f == 0)
        def _():
            y_ref[...] = part + bd_ref[...]

        @pl.when(f > 0)
        def _():
            y_ref[...] += part


def _experts(xs, block_e, n_used, wg, wu, bg, bu, wd, bd, rb):
    n_rows, d = xs.shape
    e, _, dff = wg.shape
    nb = n_rows // rb
    tf = min(512, dff)
    nf = dff // tf

    def live(j, nu):
        return jnp.minimum(j, nu[0] - 1)

    def fidx(j, f, nu):
        return jnp.where(j < nu[0], f, nf - 1)

    grid_spec = pltpu.PrefetchScalarGridSpec(
        num_scalar_prefetch=2,
        grid=(nb, nf),
        in_specs=[
            pl.BlockSpec((rb, d), lambda j, f, be, nu: (live(j, nu), 0)),
            pl.BlockSpec((None, d, tf), lambda j, f, be, nu: (be[live(j, nu)], 0, fidx(j, f, nu))),
            pl.BlockSpec((None, d, tf), lambda j, f, be, nu: (be[live(j, nu)], 0, fidx(j, f, nu))),
            pl.BlockSpec((None, 1, tf), lambda j, f, be, nu: (be[live(j, nu)], 0, fidx(j, f, nu))),
            pl.BlockSpec((None, 1, tf), lambda j, f, be, nu: (be[live(j, nu)], 0, fidx(j, f, nu))),
            pl.BlockSpec((None, tf, d), lambda j, f, be, nu: (be[live(j, nu)], fidx(j, f, nu), 0)),
            pl.BlockSpec((None, 1, d), lambda j, f, be, nu: (be[live(j, nu)], 0, 0)),
        ],
        out_specs=pl.BlockSpec((rb, d), lambda j, f, be, nu: (live(j, nu), 0)),
        scratch_shapes=[pltpu.VMEM((rb, d), BF16)],
    )
    return pl.pallas_call(
        _experts_kernel,
        grid_spec=grid_spec,
        out_shape=jax.ShapeDtypeStruct((n_rows, d), F32),
        compiler_params=_cparams("arbitrary", "arbitrary"),
        name="experts",
    )(block_e, n_used, xs, wg, wu, bg, bu, wd, bd)


def _combine_kernel(dest_ref, y_ref, w_ref, x1_ref, g2_ref, lg_ref, lb_ref, o_ref, buf_ref, sem,
                    *, alpha):
    tm = x1_ref.shape[0]

    def issue(t, c):
        for kk in range(TOP_K):
            _row_copy(y_ref, dest_ref[t * TOP_K + kk], buf_ref.at[kk], t, sem).start()
        return c

    lax.fori_loop(0, tm, issue, 0)

    def drain(t, c):
        for kk in range(TOP_K):
            _row_copy(y_ref, 0, buf_ref.at[kk], 0, sem).wait()
        return c

    lax.fori_loop(0, tm, drain, 0)
    w = w_ref[...]
    ffn = w[:, 0:1] * buf_ref[0]
    for kk in range(1, TOP_K):
        ffn = ffn + w[:, kk:kk + 1] * buf_ref[kk]
    o_ref[...] = _layernorm(alpha * x1_ref[...] + g2_ref[...] * ffn, lg_ref[...], lb_ref[...])


def _combine(dest_flat, y, top_w, x1, g2, ln_g, ln_b, t_per_batch, alpha):
    n, d = x1.shape
    tm = min(256, n)
    g2_op, g2_spec = _mod_operand(g2, n, t_per_batch, tm)
    row = lambda i: (i, 0)
    fix = lambda i: (0, 0)
    return pl.pallas_call(
        functools.partial(_combine_kernel, alpha=alpha),
        grid=(n // tm,),
        in_specs=[pl.BlockSpec((tm * TOP_K,), lambda i: (i,), memory_space=pltpu.SMEM),
                  pl.BlockSpec(memory_space=pl.ANY),
                  pl.BlockSpec((tm, TOP_K), row), pl.BlockSpec((tm, d), row), g2_spec,
                  pl.BlockSpec((1, d), fix), pl.BlockSpec((1, d), fix)],
        out_specs=pl.BlockSpec((tm, d), row),
        out_shape=jax.ShapeDtypeStruct((n, d), F32),
        scratch_shapes=[pltpu.VMEM((TOP_K, tm, d), F32), pltpu.SemaphoreType.DMA(())],
        compiler_params=_cparams("arbitrary"),
        name="combine",
    )(dest_flat, y, top_w, x1, g2_op, ln_g.reshape(1, d), ln_b.reshape(1, d))


def _moe(h2, logits, x1, g2, ln_g, ln_b, ew, t_per_batch, alpha, rb):
    n, d = h2.shape
    e = logits.shape[1]
    idx, top_w, rank, counts = _route(logits)
    padded = (counts + rb - 1) // rb * rb
    group_end = jnp.cumsum(padded, axis=1)
    dest = _dest(idx, rank, group_end - padded)
    dest_flat = dest.reshape(n * TOP_K)
    nb = -(-(n * TOP_K) // rb) + e
    block_e = jnp.minimum(jnp.searchsorted(group_end[0], jnp.arange(nb, dtype=jnp.int32) * rb,
                                           side='right'), e - 1).astype(jnp.int32)
    n_used = (group_end[0, -1:] // rb).astype(jnp.int32)
    xs = _dispatch(dest_flat, h2, nb * rb)
    y = _experts(xs, block_e, n_used, *ew, rb=rb)
    return _combine(dest_flat, y, top_w, x1, g2, ln_g, ln_b, t_per_batch, alpha)


def _layer(x, mod, s0, k_past, v_past, p, alpha):
    b, t, d = x.shape
    n = b * t
    sh1, sc1, g1, sh2, sc2, g2 = [mod[:, i * d:(i + 1) * d] for i in range(6)]
    x2 = x.reshape(n, d)
    hg, dk, dv, hs, dh = p['dims']
    g4 = _modmm(x2, sc1, sh1, p['w_g4'], t, name="proj_gla")
    la = _loga(x2, sc1, sh1, p['w_gf'], p['w_gla_f'], p['b_gla_f'], t)
    sq = _modmm(x2, sc1, sh1, p['w_sq'], t, name="proj_sq")
    sk = _modmm(x2, sc1, sh1, p['w_sk'], t, name="proj_sk")
    sv = _modmm(x2, sc1, sh1, p['w_sv'], t, name="proj_sv")
    gates = _modmm(x2, sc1, sh1, p['w_gates'], t, name="proj_gates")
    y_gla, s_fin = _gla(g4.reshape(b, t, -1), la.reshape(b, t, -1), p['gla_norm_g'], s0, hg, dk, dv)
    kp = None if k_past is None else k_past.reshape(b, k_past.shape[1], hs * dh)
    vp = None if v_past is None else v_past.reshape(b, v_past.shape[1], hs * dh)
    o_sb = _stick_breaking(sq.reshape(b, t, -1), sk.reshape(b, t, -1), sv.reshape(b, t, -1),
                           kp, vp, hs, dh)
    merged = _merge(y_gla.reshape(n, -1), o_sb.reshape(n, -1), p['w_gla_out'], p['w_sb_out'], gates)
    x1, h2, logits = _post(merged, p['w_o'], x2, g1, sc2, sh2, p['ln1_g'], p['ln1_b'],
                           p['w_router'], p['b_router'], t, alpha)
    rb = 512 if n * TOP_K >= 512 * p['n_experts'] * 4 else 128
    rb = min(rb, n)
    x_out = _moe(h2, logits, x1, g2, p['ln2_g'], p['ln2_b'], p['experts'], t, alpha, rb)
    return (x_out.reshape(b, t, d), s_fin, sk.reshape(b, t, hs, dh), sv.reshape(b, t, hs, dh))


def kernel(x_prompt, x_sample, c_prompt, c_sample, state_gla, cache_sb_k, cache_sb_v, w_ada, b_ada, w_in, w_gla_f, b_gla_f, gla_norm_g, w_gla_out, w_sb_out, w_o, ln1_g, ln1_b, w_router, b_router, w_gate_up, b_gate_up, w_down, b_down, ln2_g, ln2_b):
    depth = w_ada.shape[0]
    d = x_prompt.shape[-1]
    hg, dk, dv = state_gla.shape[2:]
    hs, dh = cache_sb_k.shape[3:]
    rank = w_gla_f.shape[1]
    n_experts = w_router.shape[-1]
    alpha = (2.0 * depth) ** 0.25
    qk, gv, sbw = hg * dk, hg * dv, hs * dh
    o_gf = 2 * qk + 2 * gv
    o_sq = o_gf + rank
    o_sk, o_sv, o_ga = o_sq + sbw, o_sq + 2 * sbw, o_sq + 3 * sbw
    rpad = -(-rank // LANES) * LANES
    n_p = x_prompt.shape[0]
    y_p, y_s = x_prompt, x_sample
    outs = [[] for _ in range(6)]
    for l in range(depth):
        wi = w_in[l]
        p = {
            'dims': (hg, dk, dv, hs, dh), 'n_experts': n_experts,
            'w_g4': wi[:, :o_gf].astype(BF16),
            'w_gf': jnp.pad(wi[:, o_gf:o_sq], ((0, 0), (0, rpad - rank))).astype(BF16),
            'w_gla_f': jnp.pad(w_gla_f[l], ((0, rpad - rank), (0, 0))).astype(BF16),
            'b_gla_f': b_gla_f[l],
            'w_sq': wi[:, o_sq:o_sk].astype(BF16), 'w_sk': wi[:, o_sk:o_sv].astype(BF16),
            'w_sv': wi[:, o_sv:o_ga].astype(BF16), 'w_gates': wi[:, o_ga:].astype(BF16),
            'gla_norm_g': gla_norm_g[l],
            'w_gla_out': w_gla_out[l].astype(BF16), 'w_sb_out': w_sb_out[l].astype(BF16),
            'w_o': w_o[l].astype(BF16), 'ln1_g': ln1_g[l], 'ln1_b': ln1_b[l],
            'w_router': w_router[l].astype(BF16), 'b_router': b_router[l],
            'ln2_g': ln2_g[l], 'ln2_b': ln2_b[l],
            'experts': (w_gate_up[l][:, :, 0::2].astype(BF16), w_gate_up[l][:, :, 1::2].astype(BF16),
                        b_gate_up[l][:, None, 0::2], b_gate_up[l][:, None, 1::2],
                        w_down[l].astype(BF16), b_down[l][:, None, :]),
        }
        mod = _ada(jnp.concatenate([c_prompt, c_sample], axis=0), w_ada[l], b_ada[l])
        y_p, sf, kn, vn = _layer(y_p, mod[:n_p], None, None, None, p, alpha)
        outs[0].append(sf), outs[1].append(kn), outs[2].append(vn)
        y_s, sf, kn, vn = _layer(y_s, mod[n_p:], state_gla[l], cache_sb_k[l], cache_sb_v[l], p, alpha)
        outs[3].append(sf), outs[4].append(kn), outs[5].append(vn)
    return (y_p, y_s) + tuple(jnp.stack(o) for o in outs)
```

```python
import functools

import jax
import jax.numpy as jnp
from jax import lax
from jax.experimental import pallas as pl
from jax.experimental.pallas import tpu as pltpu

F32 = jnp.float32
BF16 = jnp.bfloat16

GLA_CHUNK = 64
GLA_SUB = 16
GLA_TAU = 16.0
TOP_K = 4
SWIGLU_LIMIT = 7.0
SWIGLU_ALPHA = 1.702
LN_EPS = 1e-5
RMS_EPS = 1e-6
LANES = 128
VMEM_LIMIT_BYTES = 56 * 1024 * 1024


def _cparams(*sem):
    return pltpu.CompilerParams(dimension_semantics=sem, vmem_limit_bytes=VMEM_LIMIT_BYTES)


def _tile(n, want):
    t = min(want, n)
    while n % t:
        t //= 2
    return t


def _log_sigmoid(x):
    return jnp.minimum(x, 0.0) - jnp.log1p(jnp.exp(-jnp.abs(x)))


def _layernorm(u, g, b):
    mu = jnp.mean(u, axis=-1, keepdims=True)
    d = u - mu
    var = jnp.mean(d * d, axis=-1, keepdims=True)
    return d * lax.rsqrt(var + LN_EPS) * g + b


def _mod_operand(mod, n_tok, t_per_batch, tm):
    b, d = mod.shape
    if t_per_batch % tm == 0:
        per = t_per_batch // tm
        return mod.reshape(b, 1, d), pl.BlockSpec((None, 1, d), lambda i, *_: (i // per, 0, 0))
    rows = jnp.repeat(mod, t_per_batch, axis=0)
    return rows, pl.BlockSpec((tm, d), lambda i, *_: (i, 0))


def _ada_kernel(c_ref, w_ref, b_ref, o_ref):
    c = c_ref[...]
    s = c * jax.nn.sigmoid(c)
    o_ref[...] = jnp.dot(s.astype(BF16), w_ref[...].astype(BF16),
                         preferred_element_type=F32) + b_ref[...]


def _ada(c, w, b):
    bc, d = c.shape
    n = w.shape[1]
    tn = min(1024, n)
    return pl.pallas_call(
        _ada_kernel,
        grid=(n // tn,),
        in_specs=[pl.BlockSpec((bc, d), lambda j: (0, 0)),
                  pl.BlockSpec((d, tn), lambda j: (0, j)),
                  pl.BlockSpec((1, tn), lambda j: (0, j))],
        out_specs=pl.BlockSpec((bc, tn), lambda j: (0, j)),
        out_shape=jax.ShapeDtypeStruct((bc, n), F32),
        compiler_params=_cparams("parallel"),
        name="ada",
    )(c, w, b.reshape(1, n))


def _modmm_kernel(x_ref, sc_ref, sh_ref, w_ref, o_ref, hb_ref):
    @pl.when(pl.program_id(1) == 0)
    def _():
        hb_ref[...] = (x_ref[...] * (1.0 + sc_ref[...]) + sh_ref[...]).astype(BF16)

    o_ref[...] = jnp.dot(hb_ref[...], w_ref[...], preferred_element_type=F32).astype(o_ref.dtype)


def _modmm(x2, sc, sh, w, t_per_batch, out_dtype=F32, name="modmm"):
    n, d = x2.shape
    c = w.shape[1]
    tm = _tile(n, 1024)
    tn = _tile(c, 1024)
    sc_op, sc_spec = _mod_operand(sc, n, t_per_batch, tm)
    sh_op, sh_spec = _mod_operand(sh, n, t_per_batch, tm)
    return pl.pallas_call(
        _modmm_kernel,
        grid=(n // tm, c // tn),
        in_specs=[pl.BlockSpec((tm, d), lambda i, j: (i, 0)), sc_spec, sh_spec,
                  pl.BlockSpec((d, tn), lambda i, j: (0, j))],
        out_specs=pl.BlockSpec((tm, tn), lambda i, j: (i, j)),
        out_shape=jax.ShapeDtypeStruct((n, c), out_dtype),
        scratch_shapes=[pltpu.VMEM((tm, d), BF16)],
        compiler_params=_cparams("parallel", "arbitrary"),
        name=name,
    )(x2, sc_op, sh_op, w)


def _loga_kernel(x_ref, sc_ref, sh_ref, wgf_ref, wf_ref, bf_ref, o_ref):
    h = (x_ref[...] * (1.0 + sc_ref[...]) + sh_ref[...]).astype(BF16)
    gf = jnp.dot(h, wgf_ref[...], preferred_element_type=F32)
    z = jnp.dot(gf.astype(BF16), wf_ref[...], preferred_element_type=F32) + bf_ref[...]
    o_ref[...] = _log_sigmoid(z) / GLA_TAU


def _loga(x2, sc, sh, w_gf, w_f, b_f, t_per_batch):
    n, d = x2.shape
    rp = w_gf.shape[1]
    qk = w_f.shape[1]
    tm = min(512, n)
    sc_op, sc_spec = _mod_operand(sc, n, t_per_batch, tm)
    sh_op, sh_spec = _mod_operand(sh, n, t_per_batch, tm)
    return pl.pallas_call(
        _loga_kernel,
        grid=(n // tm,),
        in_specs=[pl.BlockSpec((tm, d), lambda i: (i, 0)), sc_spec, sh_spec,
                  pl.BlockSpec((d, rp), lambda i: (0, 0)),
                  pl.BlockSpec((rp, qk), lambda i: (0, 0)),
                  pl.BlockSpec((1, qk), lambda i: (0, 0))],
        out_specs=pl.BlockSpec((tm, qk), lambda i: (i, 0)),
        out_shape=jax.ShapeDtypeStruct((n, qk), F32),
        compiler_params=_cparams("parallel"),
        name="loga",
    )(x2, sc_op, sh_op, w_gf, w_f, b_f.reshape(1, qk))


def _gla_kernel(*refs, chunk, n_chunks, has_s0, q_scale):
    if has_s0:
        q_ref, k_ref, v_ref, gg_ref, la_ref, gn_ref, s0_ref, y_ref, sfin_ref, st_ref = refs
    else:
        q_ref, k_ref, v_ref, gg_ref, la_ref, gn_ref, y_ref, sfin_ref, st_ref = refs
    c_idx = pl.program_id(2)
    dk = q_ref.shape[-1]
    n_sub = chunk // GLA_SUB

    @pl.when(c_idx == 0)
    def _():
        if has_s0:
            st_ref[...] = s0_ref[...].T
        else:
            st_ref[...] = jnp.zeros_like(st_ref)

    row_c = lax.broadcasted_iota(jnp.int32, (chunk, chunk), 0)
    col_c = lax.broadcasted_iota(jnp.int32, (chunk, chunk), 1)
    tri_incl = (row_c >= col_c).astype(F32)
    row_s = lax.broadcasted_iota(jnp.int32, (GLA_SUB, GLA_SUB), 0)
    col_s = lax.broadcasted_iota(jnp.int32, (GLA_SUB, GLA_SUB), 1)
    gnorm = gn_ref[...]

    def chunk_body(ci, carry):
        r0 = pl.multiple_of(ci * chunk, chunk)
        q = q_ref[pl.ds(r0, chunk), :] * q_scale
        k = k_ref[pl.ds(r0, chunk), :]
        v = v_ref[pl.ds(r0, chunk), :]
        la = la_ref[pl.ds(r0, chunk), :]
        vb = v.astype(BF16)
        cum = jnp.dot(tri_incl, la, preferred_element_type=F32, precision=lax.Precision.HIGHEST)
        st = st_ref[...]
        o = lax.dot_general((q * jnp.exp(cum)).astype(BF16), st.astype(BF16),
                            (((1,), (1,)), ((), ())), preferred_element_type=F32)
        o_rows = []
        for i in range(n_sub):
            lo = i * GLA_SUB
            qi, ki, ci_ = q[lo:lo + GLA_SUB], k[lo:lo + GLA_SUB], cum[lo:lo + GLA_SUB]
            sd = jnp.zeros((GLA_SUB, GLA_SUB), F32)
            for j in range(GLA_SUB):
                e = jnp.exp(jnp.minimum(ci_ - ci_[j:j + 1], 0.0))
                col = jnp.sum(qi * ki[j:j + 1] * e, axis=1, keepdims=True)
                sd = jnp.where(col_s == j, col, sd)
            sd = jnp.where(row_s >= col_s, sd, 0.0)
            oi = jnp.dot(sd.astype(BF16), vb[lo:lo + GLA_SUB], preferred_element_type=F32)
            if i > 0:
                ref_row = cum[lo - 1:lo]
                qd = (qi * jnp.exp(ci_ - ref_row)).astype(BF16)
                kd = (k[:lo] * jnp.exp(ref_row - cum[:lo])).astype(BF16)
                so = lax.dot_general(qd, kd, (((1,), (1,)), ((), ())), preferred_element_type=F32)
                oi = oi + jnp.dot(so.astype(BF16), vb[:lo], preferred_element_type=F32)
            o_rows.append(oi)
        o = o + (jnp.concatenate(o_rows, axis=0) if n_sub > 1 else o_rows[0])
        last = cum[chunk - 1:chunk]
        kdec = (k * jnp.exp(last - cum)).astype(BF16)
        st_ref[...] = st * jnp.exp(last) + lax.dot_general(
            vb, kdec, (((0,), (0,)), ((), ())), preferred_element_type=F32)
        gg = gg_ref[pl.ds(r0, chunk), :]
        rms = lax.rsqrt(jnp.mean(o * o, axis=-1, keepdims=True) + RMS_EPS)
        y_ref[pl.ds(r0, chunk), :] = (o * rms * gnorm * (gg * jax.nn.sigmoid(gg))).astype(y_ref.dtype)
        return carry

    lax.fori_loop(0, n_chunks, chunk_body, 0, unroll=2 if n_chunks % 2 == 0 else 1)

    @pl.when(c_idx == pl.num_programs(2) - 1)
    def _():
        sfin_ref[...] = st_ref[...].T


def _gla(g4, la, gnorm, s0, n_heads, dk, dv):
    b, t, _ = g4.shape
    chunk = min(GLA_CHUNK, t)
    tc = min(512, t)
    n_chunks = tc // chunk
    has_s0 = s0 is not None
    kq = (n_heads * dk) // dk
    kv = (2 * n_heads * dk) // dv
    kg = kv + n_heads
    in_specs = [pl.BlockSpec((None, tc, dk), lambda bi, h, c: (bi, c, h)),
                pl.BlockSpec((None, tc, dk), lambda bi, h, c: (bi, c, kq + h)),
                pl.BlockSpec((None, tc, dv), lambda bi, h, c: (bi, c, kv + h)),
                pl.BlockSpec((None, tc, dv), lambda bi, h, c: (bi, c, kg + h)),
                pl.BlockSpec((None, tc, dk), lambda bi, h, c: (bi, c, h)),
                pl.BlockSpec((1, dv), lambda bi, h, c: (0, 0))]
    args = [g4, g4, g4, g4, la, gnorm.reshape(1, dv)]
    if has_s0:
        in_specs.append(pl.BlockSpec((None, None, dk, dv), lambda bi, h, c: (bi, h, 0, 0)))
        args.append(s0)
    kern = functools.partial(_gla_kernel, chunk=chunk, n_chunks=n_chunks, has_s0=has_s0,
                             q_scale=dk ** -0.5)
    return pl.pallas_call(
        kern,
        grid=(b, n_heads, t // tc),
        in_specs=in_specs,
        out_specs=[pl.BlockSpec((None, tc, dv), lambda bi, h, c: (bi, c, h)),
                   pl.BlockSpec((None, None, dk, dv), lambda bi, h, c: (bi, h, 0, 0))],
        out_shape=[jax.ShapeDtypeStruct((b, t, n_heads * dv), BF16),
                   jax.ShapeDtypeStruct((b, n_heads, dk, dv), F32)],
        scratch_shapes=[pltpu.VMEM((dv, dk), F32)],
        compiler_params=_cparams("parallel", "parallel", "arbitrary"),
        name="gla",
    )(*args)


SB_GROUP = 8


def _sb_split_dot(log_1m, u, u_first):
    x = log_1m.astype(BF16)
    if u_first:
        return jnp.dot(u, x, preferred_element_type=F32)
    return jnp.dot(x, u, preferred_element_type=F32)


def _sb_log_terms(z):
    log_beta = jnp.minimum(z, 0.0) - jnp.log(1.0 + jnp.exp(-jnp.abs(z)))
    return log_beta, log_beta - z


def _sb_scores(qb, kb, u, masked):
    tq, tk = qb.shape[0], kb.shape[0]
    z = lax.dot_general(qb, kb, (((1,), (1,)), ((), ())), preferred_element_type=F32)
    log_beta, log_1m = _sb_log_terms(z)
    vis = None
    if masked:
        vis = (lax.broadcasted_iota(jnp.int32, (tq, tk), 1)
               < lax.broadcasted_iota(jnp.int32, (tq, tk), 0))
        log_1m = jnp.where(vis, log_1m, 0.0)
    later = _sb_split_dot(log_1m, u, False)
    return log_beta, later, later[:, :1] + log_1m[:, :1], vis


def _sb_group(qb, tiles, run, acc, u, masked):
    parts = [_sb_scores(qb, kb, u, masked) for kb, _ in tiles]
    for (log_beta, later, total, vis), (_, vb) in zip(parts, tiles):
        w = jnp.exp(log_beta + (later + run))
        if masked:
            w = jnp.where(vis, w, 0.0)
        acc = acc + jnp.dot(w.astype(BF16), vb, preferred_element_type=F32)
        run = run + total
    return run, acc


def _sb_kernel(q_ref, kn_ref, vn_ref, u_ref, o_ref, kb_ref, vb_ref, *, tq, scale):
    i = pl.program_id(2)
    t, dh = kn_ref.shape
    ct = min(512, t)

    @pl.when(i == 0)
    def _():
        def cast(c, carry):
            r = pl.multiple_of(c * ct, ct)
            kb_ref[pl.ds(r, ct), :] = kn_ref[pl.ds(r, ct), :].astype(BF16)
            vb_ref[pl.ds(r, ct), :] = vn_ref[pl.ds(r, ct), :].astype(BF16)
            return carry

        lax.fori_loop(0, t // ct, cast, 0)

    qb = (q_ref[...] * scale).astype(BF16)
    u = u_ref[...]

    def tile(kt):
        r = pl.multiple_of(kt * tq, tq)
        return kb_ref[pl.ds(r, tq), :], vb_ref[pl.ds(r, tq), :]

    run = jnp.zeros((tq, 1), F32)
    acc = jnp.zeros((tq, dh), F32)
    run, acc = _sb_group(qb, [tile(i)], run, acc, u, True)
    top = i - 1
    size = 1
    while size <= SB_GROUP:
        full = size == SB_GROUP
        count = lax.div(i, size) if full else lax.rem(lax.div(i, size), 2)

        def group(n, c, size=size, top=top):
            first = top - n * size
            return _sb_group(qb, [tile(first - j) for j in range(size)], c[0], c[1], u, False)

        run, acc = lax.fori_loop(0, count, group, (run, acc))
        top = top - count * size
        size *= 2
    o_ref[...] = acc.astype(o_ref.dtype)


def _stick_breaking(q, k_new, v_new, n_heads, dh):
    b, t, _ = q.shape
    tq = min(256, t)
    u = (lax.broadcasted_iota(jnp.int32, (tq, tq), 0)
         > lax.broadcasted_iota(jnp.int32, (tq, tq), 1)).astype(BF16)
    kern = functools.partial(_sb_kernel, tq=tq, scale=dh ** -0.5)
    return pl.pallas_call(
        kern,
        grid=(b, n_heads, t // tq),
        in_specs=[pl.BlockSpec((None, tq, dh), lambda bi, h, i: (bi, i, h)),
                  pl.BlockSpec((None, t, dh), lambda bi, h, i: (bi, 0, h)),
                  pl.BlockSpec((None, t, dh), lambda bi, h, i: (bi, 0, h)),
                  pl.BlockSpec((tq, tq), lambda bi, h, i: (0, 0))],
        out_specs=pl.BlockSpec((None, tq, dh), lambda bi, h, i: (bi, i, h)),
        out_shape=jax.ShapeDtypeStruct((b, t, n_heads * dh), BF16),
        scratch_shapes=[pltpu.VMEM((t, dh), BF16), pltpu.VMEM((t, dh), BF16)],
        compiler_params=_cparams("parallel", "parallel", "arbitrary"),
        name="stick_breaking",
    )(q, k_new, v_new, u)


def _sbd_kernel(q_ref, kn_ref, vn_ref, kc_ref, vc_ref, u_ref, qpos_ref, o_ref,
                qbd_ref, run_ref, acc_ref, *, n_heads, dh, t, tk, hg, scale):
    s = pl.program_id(1)

    @pl.when(s == 0)
    def _():
        qbd_ref[...] = jnp.zeros_like(qbd_ref)
        for h in range(n_heads):
            qbd_ref[h * t:(h + 1) * t, h * dh:(h + 1) * dh] = (
                q_ref[:, h * dh:(h + 1) * dh] * scale).astype(BF16)
        run_ref[...] = jnp.zeros_like(run_ref)
        acc_ref[...] = jnp.zeros_like(acc_ref)

    def tile(get_k, get_v, masked):
        zt = None
        for g in range(n_heads // hg):
            kcat = [get_k(g * hg + j) for j in range(hg)]
            kcat = (jnp.concatenate(kcat, axis=1) if hg > 1 else kcat[0]).astype(BF16)
            part = lax.dot_general(kcat, qbd_ref[:, g * hg * dh:(g + 1) * hg * dh],
                                   (((1,), (1,)), ((), ())), preferred_element_type=F32)
            zt = part if zt is None else zt + part
        log_beta, log_1m = _sb_log_terms(zt)
        vis = None
        if masked:
            vis = lax.broadcasted_iota(jnp.int32, zt.shape, 0) < qpos_ref[...]
            log_1m = jnp.where(vis, log_1m, 0.0)
        later = _sb_split_dot(log_1m, u_ref[...], True)
        w = jnp.exp(log_beta + (later + run_ref[...]))
        if masked:
            w = jnp.where(vis, w, 0.0)
        run_ref[...] += later[0:1, :] + log_1m[0:1, :]
        wt = w.T
        for h in range(n_heads):
            acc_ref[:, h * dh:(h + 1) * dh] += jnp.dot(
                wt[h * t:(h + 1) * t, :].astype(BF16), get_v(h).astype(BF16),
                preferred_element_type=F32)

    @pl.when(s == 0)
    def _():
        pad = jnp.zeros((tk - t, dh), F32)
        tile(lambda h: jnp.concatenate([kn_ref[:, h * dh:(h + 1) * dh], pad], axis=0),
             lambda h: jnp.concatenate([vn_ref[:, h * dh:(h + 1) * dh], pad], axis=0), True)

    @pl.when(s > 0)
    def _():
        tile(lambda h: kc_ref[pl.ds(h, tk, stride=n_heads), :],
             lambda h: vc_ref[pl.ds(h, tk, stride=n_heads), :], False)

    @pl.when(s == pl.num_programs(1) - 1)
    def _():
        o_ref[...] = acc_ref[...].astype(o_ref.dtype)


def _stick_breaking_decode(q, k_new, v_new, cache_k, cache_v, layer, n_heads, dh):
    b, t, hd = q.shape
    tp = cache_k.shape[2]
    tk = min(256, tp)
    n_ct = tp // tk
    nq = n_heads * t
    nqp = -(-nq // LANES) * LANES
    hg = max(1, 2 * LANES // dh)
    if n_heads % hg:
        hg = 1
    u = (lax.broadcasted_iota(jnp.int32, (tk, tk), 1)
         > lax.broadcasted_iota(jnp.int32, (tk, tk), 0)).astype(BF16)
    col = jnp.arange(nqp, dtype=jnp.int32)
    qpos = jnp.where(col < nq, col % t, 0).reshape(1, nqp)
    ck = cache_k.reshape(cache_k.shape[0], b, tp * n_heads, dh)
    cv = cache_v.reshape(cache_v.shape[0], b, tp * n_heads, dh)
    cache_spec = pl.BlockSpec((None, None, tk * n_heads, dh),
                              lambda bi, s: (layer, bi, jnp.minimum(n_ct - s, n_ct - 1), 0))
    new_spec = pl.BlockSpec((None, t, hd), lambda bi, s: (bi, 0, 0))
    kern = functools.partial(_sbd_kernel, n_heads=n_heads, dh=dh, t=t, tk=tk, hg=hg, scale=dh ** -0.5)
    return pl.pallas_call(
        kern,
        grid=(b, n_ct + 1),
        in_specs=[new_spec, new_spec, new_spec, cache_spec, cache_spec,
                  pl.BlockSpec((tk, tk), lambda bi, s: (0, 0)),
                  pl.BlockSpec((1, nqp), lambda bi, s: (0, 0))],
        out_specs=new_spec,
        out_shape=jax.ShapeDtypeStruct((b, t, hd), BF16),
        scratch_shapes=[pltpu.VMEM((nqp, hd), BF16), pltpu.VMEM((1, nqp), F32),
                        pltpu.VMEM((t, hd), F32)],
        compiler_params=_cparams("parallel", "arbitrary"),
        name="stick_breaking_decode",
    )(q, k_new, v_new, ck, cv, u, qpos)


def _merge_kernel(ya_ref, yb_ref, wa_ref, wb_ref, ga_ref, gb_ref, o_ref):
    a = jnp.dot(ya_ref[...], wa_ref[...], preferred_element_type=F32)
    bb = jnp.dot(yb_ref[...], wb_ref[...], preferred_element_type=F32)
    o_ref[...] = (jax.nn.sigmoid(ga_ref[...]) * a + jax.nn.sigmoid(gb_ref[...]) * bb).astype(o_ref.dtype)


def _merge(ya, yb, wa, wb, gates):
    n, da = ya.shape
    db = yb.shape[1]
    d = wa.shape[1]
    tm = min(1024, n)
    tn = min(512, d)
    nj = d // tn
    return pl.pallas_call(
        _merge_kernel,
        grid=(n // tm, nj),
        in_specs=[pl.BlockSpec((tm, da), lambda i, j: (i, 0)),
                  pl.BlockSpec((tm, db), lambda i, j: (i, 0)),
                  pl.BlockSpec((da, tn), lambda i, j: (0, j)),
                  pl.BlockSpec((db, tn), lambda i, j: (0, j)),
                  pl.BlockSpec((tm, tn), lambda i, j: (i, j)),
                  pl.BlockSpec((tm, tn), lambda i, j: (i, nj + j))],
        out_specs=pl.BlockSpec((tm, tn), lambda i, j: (i, j)),
        out_shape=jax.ShapeDtypeStruct((n, d), BF16),
        compiler_params=_cparams("parallel", "arbitrary"),
        name="merge",
    )(ya, yb, wa, wb, gates, gates)


def _post_kernel(m_ref, wo_ref, x_ref, g1_ref, sc_ref, sh_ref, lg_ref, lb_ref, wr_ref, br_ref,
                 x1_ref, h2_ref, lo_ref, *, alpha):
    mix = jnp.dot(m_ref[...], wo_ref[...], preferred_element_type=F32)
    x1 = _layernorm(alpha * x_ref[...] + g1_ref[...] * mix, lg_ref[...], lb_ref[...])
    h2 = x1 * (1.0 + sc_ref[...]) + sh_ref[...]
    x1_ref[...] = x1
    h2_ref[...] = h2
    lo_ref[...] = jnp.dot(h2.astype(BF16), wr_ref[...], preferred_element_type=F32) + br_ref[...]


def _post(merged, wo, x2, g1, sc2, sh2, ln_g, ln_b, wr, br, t_per_batch, alpha):
    n, d = x2.shape
    e = wr.shape[1]
    tm = min(256, n)
    g1_op, g1_spec = _mod_operand(g1, n, t_per_batch, tm)
    sc_op, sc_spec = _mod_operand(sc2, n, t_per_batch, tm)
    sh_op, sh_spec = _mod_operand(sh2, n, t_per_batch, tm)
    row = lambda i: (i, 0)
    fix = lambda i: (0, 0)
    return pl.pallas_call(
        functools.partial(_post_kernel, alpha=alpha),
        grid=(n // tm,),
        in_specs=[pl.BlockSpec((tm, d), row), pl.BlockSpec((d, d), fix), pl.BlockSpec((tm, d), row),
                  g1_spec, sc_spec, sh_spec,
                  pl.BlockSpec((1, d), fix), pl.BlockSpec((1, d), fix),
                  pl.BlockSpec((d, e), fix), pl.BlockSpec((1, e), fix)],
        out_specs=[pl.BlockSpec((tm, d), row), pl.BlockSpec((tm, d), row), pl.BlockSpec((tm, e), row)],
        out_shape=[jax.ShapeDtypeStruct((n, d), F32), jax.ShapeDtypeStruct((n, d), F32),
                   jax.ShapeDtypeStruct((n, e), F32)],
        compiler_params=_cparams("parallel"),
        name="post",
    )(merged, wo, x2, g1_op, sc_op, sh_op, ln_g.reshape(1, d), ln_b.reshape(1, d), wr,
      br.reshape(1, e))


def _cols_to_lanes(cols):
    tm = cols[0].shape[0]
    lane = lax.broadcasted_iota(jnp.int32, (tm, len(cols)), 1)
    out = jnp.broadcast_to(cols[0], (tm, len(cols)))
    for kk in range(1, len(cols)):
        out = jnp.where(lane == kk, cols[kk], out)
    return out


def _route_kernel(lo_ref, idx_ref, w_ref, rank_ref, cnt_ref, carry_ref):
    @pl.when(pl.program_id(0) == 0)
    def _():
        carry_ref[...] = jnp.zeros_like(carry_ref)

    lg = lo_ref[...]
    tm, e = lg.shape
    lane = lax.broadcasted_iota(jnp.int32, (tm, e), 1).astype(F32)
    vals, idxs = [], []
    chosen = jnp.zeros((tm, e), F32)
    for _ in range(TOP_K):
        m = jnp.max(lg, axis=-1, keepdims=True)
        ik = jnp.min(jnp.where(lg == m, lane, float(e)), axis=-1, keepdims=True)
        sel = lane == ik
        vals.append(m)
        idxs.append(ik)
        chosen = jnp.where(sel, 1.0, chosen)
        lg = jnp.where(sel, -jnp.inf, lg)
    ex = [jnp.exp(v - vals[0]) for v in vals]
    tot = ex[0]
    for t_ in ex[1:]:
        tot = tot + t_
    before = (lax.broadcasted_iota(jnp.int32, (tm, tm), 1)
              < lax.broadcasted_iota(jnp.int32, (tm, tm), 0)).astype(BF16)
    pos = jnp.dot(before, chosen.astype(BF16), preferred_element_type=F32) + carry_ref[...]
    ranks = [jnp.sum(jnp.where(lane == ik, pos, 0.0), axis=-1, keepdims=True) for ik in idxs]
    idx_ref[...] = _cols_to_lanes(idxs).astype(jnp.int32)
    w_ref[...] = _cols_to_lanes([x / tot for x in ex])
    rank_ref[...] = _cols_to_lanes(ranks).astype(jnp.int32)
    carry_ref[...] = carry_ref[...] + jnp.sum(chosen, axis=0, keepdims=True)
    cnt_ref[...] = carry_ref[...].astype(jnp.int32)


def _route(logits):
    n, e = logits.shape
    tm = min(256, n)
    row = lambda i: (i, 0)
    return pl.pallas_call(
        _route_kernel,
        grid=(n // tm,),
        in_specs=[pl.BlockSpec((tm, e), row)],
        out_specs=[pl.BlockSpec((tm, TOP_K), row), pl.BlockSpec((tm, TOP_K), row),
                   pl.BlockSpec((tm, TOP_K), row), pl.BlockSpec((1, e), lambda i: (0, 0))],
        out_shape=[jax.ShapeDtypeStruct((n, TOP_K), jnp.int32), jax.ShapeDtypeStruct((n, TOP_K), F32),
                   jax.ShapeDtypeStruct((n, TOP_K), jnp.int32), jax.ShapeDtypeStruct((1, e), jnp.int32)],
        scratch_shapes=[pltpu.VMEM((1, e), F32)],
        compiler_params=_cparams("arbitrary"),
        name="route",
    )(logits)


def _dest_kernel(idx_ref, rank_ref, start_ref, o_ref):
    idx = idx_ref[...]
    tm = idx.shape[0]
    e = start_ref.shape[1]
    lane = lax.broadcasted_iota(jnp.int32, (tm, e), 1)
    start = start_ref[...].astype(F32)
    cols = [jnp.sum(jnp.where(lane == idx[:, kk:kk + 1], start, 0.0), axis=-1, keepdims=True)
            for kk in range(TOP_K)]
    o_ref[...] = _cols_to_lanes(cols).astype(jnp.int32) + rank_ref[...]


def _dest(idx, rank, group_start):
    n = idx.shape[0]
    e = group_start.shape[1]
    tm = min(256, n)
    row = lambda i: (i, 0)
    return pl.pallas_call(
        _dest_kernel,
        grid=(n // tm,),
        in_specs=[pl.BlockSpec((tm, TOP_K), row), pl.BlockSpec((tm, TOP_K), row),
                  pl.BlockSpec((1, e), lambda i: (0, 0))],
        out_specs=pl.BlockSpec((tm, TOP_K), row),
        out_shape=jax.ShapeDtypeStruct((n, TOP_K), jnp.int32),
        compiler_params=_cparams("parallel"),
        name="dest",
    )(idx, rank, group_start)


def _split_gate_up_kernel(w_ref, p_ref, g_ref, u_ref):
    x = w_ref[...].astype(BF16)
    p = p_ref[...]
    width = p.shape[0]
    half = width // 2
    for g in range(x.shape[1] // width):
        res = jnp.dot(x[:, g * width:(g + 1) * width], p, preferred_element_type=F32)
        g_ref[:, g * half:(g + 1) * half] = res[:, :half].astype(BF16)
        u_ref[:, g * half:(g + 1) * half] = res[:, half:].astype(BF16)


def _split_gate_up(w):
    e, d, f2 = w.shape
    width = 2 * LANES
    tr = min(1024, d)
    tc = min(1024, f2)
    r = lax.broadcasted_iota(jnp.int32, (width, width), 0)
    c = lax.broadcasted_iota(jnp.int32, (width, width), 1)
    perm = jnp.where(c < LANES, r == 2 * c, r == 2 * (c - LANES) + 1).astype(BF16)
    return pl.pallas_call(
        _split_gate_up_kernel,
        grid=(e, d // tr, f2 // tc),
        in_specs=[pl.BlockSpec((None, tr, tc), lambda ei, ri, ci: (ei, ri, ci)),
                  pl.BlockSpec((width, width), lambda ei, ri, ci: (0, 0))],
        out_specs=[pl.BlockSpec((None, tr, tc // 2), lambda ei, ri, ci: (ei, ri, ci)),
                   pl.BlockSpec((None, tr, tc // 2), lambda ei, ri, ci: (ei, ri, ci))],
        out_shape=[jax.ShapeDtypeStruct((e, d, f2 // 2), BF16),
                   jax.ShapeDtypeStruct((e, d, f2 // 2), BF16)],
        compiler_params=_cparams("parallel", "parallel", "parallel"),
        name="split_gate_up",
    )(w, perm)


def _row_copy(src_ref, src_row, dst_ref, dst_row, sem):
    return pltpu.make_async_copy(src_ref.at[pl.ds(src_row, 1), :], dst_ref.at[pl.ds(dst_row, 1), :], sem)


def _dispatch_kernel(dest_ref, h_ref, xs_in_ref, xs_ref, sem):
    del xs_in_ref
    tm = h_ref.shape[0]

    def issue(t, c):
        for kk in range(TOP_K):
            _row_copy(h_ref, t, xs_ref, dest_ref[t * TOP_K + kk], sem).start()
        return c

    lax.fori_loop(0, tm, issue, 0)

    def drain(t, c):
        for kk in range(TOP_K):
            _row_copy(h_ref, 0, xs_ref, 0, sem).wait()
        return c

    lax.fori_loop(0, tm, drain, 0)


def _dispatch(dest_flat, h2, n_rows):
    n, d = h2.shape
    tm = min(256, n)
    xs0 = jnp.zeros((n_rows, d), F32)
    return pl.pallas_call(
        _dispatch_kernel,
        grid=(n // tm,),
        in_specs=[pl.BlockSpec((tm * TOP_K,), lambda i: (i,), memory_space=pltpu.SMEM),
                  pl.BlockSpec((tm, d), lambda i: (i, 0)),
                  pl.BlockSpec(memory_space=pl.ANY)],
        out_specs=pl.BlockSpec(memory_space=pl.ANY),
        out_shape=jax.ShapeDtypeStruct((n_rows, d), F32),
        scratch_shapes=[pltpu.SemaphoreType.DMA(())],
        input_output_aliases={2: 0},
        compiler_params=_cparams("arbitrary"),
        name="dispatch",
    )(dest_flat, h2, xs0)


def _experts_kernel(be_ref, nu_ref, xs_ref, wg_ref, wu_ref, bg_ref, bu_ref, wd_ref, bd_ref,
                    y_ref, xb_ref):
    del be_ref
    j = pl.program_id(0)
    f = pl.program_id(1)

    @pl.when(j < nu_ref[0])
    def _():
        @pl.when(f == 0)
        def _():
            xb_ref[...] = xs_ref[...].astype(BF16)

        xb = xb_ref[...]
        gate = jnp.dot(xb, wg_ref[...], preferred_element_type=F32) + bg_ref[...]
        up = jnp.dot(xb, wu_ref[...], preferred_element_type=F32) + bu_ref[...]
        gate = jnp.minimum(gate, SWIGLU_LIMIT)
        up = jnp.clip(up, -SWIGLU_LIMIT, SWIGLU_LIMIT)
        act = (up + 1.0) * (gate * jax.nn.sigmoid(SWIGLU_ALPHA * gate))
        part = jnp.dot(act.astype(BF16), wd_ref[...], preferred_element_type=F32)

        @pl.when(f == 0)
        def _():
            y_ref[...] = part + bd_ref[...]

        @pl.when(f > 0)
        def _():
            y_ref[...] += part

    @pl.when(jnp.logical_and(j >= nu_ref[0], f == 0))
    def _():
        y_ref[...] = jnp.zeros_like(y_ref)


def _experts(xs, block_e, n_used, wg, wu, bg, bu, wd, bd, rb):
    n_rows, d = xs.shape
    e, _, dff = wg.shape
    nb = n_rows // rb
    tf = min(512, dff)
    nf = dff // tf

    def live(j, nu):
        return jnp.minimum(j, nu[0] - 1)

    def fidx(j, f, nu):
        return jnp.where(j < nu[0], f, nf - 1)

    grid_spec = pltpu.PrefetchScalarGridSpec(
        num_scalar_prefetch=2,
        grid=(nb, nf),
        in_specs=[
            pl.BlockSpec((rb, d), lambda j, f, be, nu: (live(j, nu), 0)),
            pl.BlockSpec((None, d, tf), lambda j, f, be, nu: (be[live(j, nu)], 0, fidx(j, f, nu))),
            pl.BlockSpec((None, d, tf), lambda j, f, be, nu: (be[live(j, nu)], 0, fidx(j, f, nu))),
            pl.BlockSpec((None, 1, tf), lambda j, f, be, nu: (be[live(j, nu)], 0, fidx(j, f, nu))),
            pl.BlockSpec((None, 1, tf), lambda j, f, be, nu: (be[live(j, nu)], 0, fidx(j, f, nu))),
            pl.BlockSpec((None, tf, d), lambda j, f, be, nu: (be[live(j, nu)], fidx(j, f, nu), 0)),
            pl.BlockSpec((None, 1, d), lambda j, f, be, nu: (be[live(j, nu)], 0, 0)),
        ],
        out_specs=pl.BlockSpec((rb, d), lambda j, f, be, nu: (j, 0)),
        scratch_shapes=[pltpu.VMEM((rb, d), BF16)],
    )
    return pl.pallas_call(
        _experts_kernel,
        grid_spec=grid_spec,
        out_shape=jax.ShapeDtypeStruct((n_rows, d), F32),
        compiler_params=_cparams("arbitrary", "arbitrary"),
        name="experts",
    )(block_e, n_used, xs, wg, wu, bg, bu, wd, bd)


def _combine_kernel(dest_ref, y_ref, w_ref, x1_ref, g2_ref, lg_ref, lb_ref, o_ref, buf_ref, sem,
                    *, alpha):
    tm = x1_ref.shape[0]

    def issue(t, c):
        for kk in range(TOP_K):
            _row_copy(y_ref, dest_ref[t * TOP_K + kk], buf_ref.at[kk], t, sem).start()
        return c

    lax.fori_loop(0, tm, issue, 0)

    def drain(t, c):
        for kk in range(TOP_K):
            _row_copy(y_ref, 0, buf_ref.at[kk], 0, sem).wait()
        return c

    lax.fori_loop(0, tm, drain, 0)
    w = w_ref[...]
    ffn = w[:, 0:1] * buf_ref[0]
    for kk in range(1, TOP_K):
        ffn = ffn + w[:, kk:kk + 1] * buf_ref[kk]
    o_ref[...] = _layernorm(alpha * x1_ref[...] + g2_ref[...] * ffn, lg_ref[...], lb_ref[...])


def _combine(dest_flat, y, top_w, x1, g2, ln_g, ln_b, t_per_batch, alpha):
    n, d = x1.shape
    tm = min(256, n)
    g2_op, g2_spec = _mod_operand(g2, n, t_per_batch, tm)
    row = lambda i: (i, 0)
    fix = lambda i: (0, 0)
    return pl.pallas_call(
        functools.partial(_combine_kernel, alpha=alpha),
        grid=(n // tm,),
        in_specs=[pl.BlockSpec((tm * TOP_K,), lambda i: (i,), memory_space=pltpu.SMEM),
                  pl.BlockSpec(memory_space=pl.ANY),
                  pl.BlockSpec((tm, TOP_K), row), pl.BlockSpec((tm, d), row), g2_spec,
                  pl.BlockSpec((1, d), fix), pl.BlockSpec((1, d), fix)],
        out_specs=pl.BlockSpec((tm, d), row),
        out_shape=jax.ShapeDtypeStruct((n, d), F32),
        scratch_shapes=[pltpu.VMEM((TOP_K, tm, d), F32), pltpu.SemaphoreType.DMA(())],
        compiler_params=_cparams("arbitrary"),
        name="combine",
    )(dest_flat, y, top_w, x1, g2_op, ln_g.reshape(1, d), ln_b.reshape(1, d))


def _moe(h2, logits, x1, g2, ln_g, ln_b, ew, t_per_batch, alpha, rb):
    n, d = h2.shape
    e = logits.shape[1]
    idx, top_w, rank, counts = _route(logits)
    padded = (counts + rb - 1) // rb * rb
    group_end = jnp.cumsum(padded, axis=1)
    dest = _dest(idx, rank, group_end - padded)
    dest_flat = dest.reshape(n * TOP_K)
    nb = -(-(n * TOP_K) // rb) + e
    block_e = jnp.minimum(jnp.searchsorted(group_end[0], jnp.arange(nb, dtype=jnp.int32) * rb,
                                           side='right'), e - 1).astype(jnp.int32)
    n_used = (group_end[0, -1:] // rb).astype(jnp.int32)
    xs = _dispatch(dest_flat, h2, nb * rb)
    y = _experts(xs, block_e, n_used, *ew, rb=rb)
    return _combine(dest_flat, y, top_w, x1, g2, ln_g, ln_b, t_per_batch, alpha)


def _layer(x, mod, s0, cache, p, alpha):
    b, t, d = x.shape
    n = b * t
    sh1, sc1, g1, sh2, sc2, g2 = [mod[:, i * d:(i + 1) * d] for i in range(6)]
    x2 = x.reshape(n, d)
    hg, dk, dv, hs, dh = p['dims']
    g4 = _modmm(x2, sc1, sh1, p['w_g4'], t, name="proj_gla")
    la = _loga(x2, sc1, sh1, p['w_gf'], p['w_gla_f'], p['b_gla_f'], t)
    sq = _modmm(x2, sc1, sh1, p['w_sq'], t, name="proj_sq")
    sk = _modmm(x2, sc1, sh1, p['w_sk'], t, name="proj_sk")
    sv = _modmm(x2, sc1, sh1, p['w_sv'], t, name="proj_sv")
    gates = _modmm(x2, sc1, sh1, p['w_gates'], t, name="proj_gates")
    y_gla, s_fin = _gla(g4.reshape(b, t, -1), la.reshape(b, t, -1), p['gla_norm_g'], s0, hg, dk, dv)
    q3, k3, v3 = sq.reshape(b, t, -1), sk.reshape(b, t, -1), sv.reshape(b, t, -1)
    if cache is None:
        o_sb = _stick_breaking(q3, k3, v3, hs, dh)
    else:
        o_sb = _stick_breaking_decode(q3, k3, v3, cache[0], cache[1], cache[2], hs, dh)
    merged = _merge(y_gla.reshape(n, -1), o_sb.reshape(n, -1), p['w_gla_out'], p['w_sb_out'], gates)
    x1, h2, logits = _post(merged, p['w_o'], x2, g1, sc2, sh2, p['ln1_g'], p['ln1_b'],
                           p['w_router'], p['b_router'], t, alpha)
    rb = 512 if n * TOP_K >= 512 * p['n_experts'] * 4 else 128
    rb = min(rb, n)
    x_out = _moe(h2, logits, x1, g2, p['ln2_g'], p['ln2_b'], p['experts'], t, alpha, rb)
    return (x_out.reshape(b, t, d), s_fin, sk.reshape(b, t, hs, dh), sv.reshape(b, t, hs, dh))


def kernel(x_prompt, x_sample, c_prompt, c_sample, state_gla, cache_sb_k, cache_sb_v, w_ada, b_ada, w_in, w_gla_f, b_gla_f, gla_norm_g, w_gla_out, w_sb_out, w_o, ln1_g, ln1_b, w_router, b_router, w_gate_up, b_gate_up, w_down, b_down, ln2_g, ln2_b):
    depth = w_ada.shape[0]
    d = x_prompt.shape[-1]
    hg, dk, dv = state_gla.shape[2:]
    hs, dh = cache_sb_k.shape[3:]
    rank = w_gla_f.shape[1]
    n_experts = w_router.shape[-1]
    alpha = (2.0 * depth) ** 0.25
    qk, gv, sbw = hg * dk, hg * dv, hs * dh
    o_gf = 2 * qk + 2 * gv
    o_sq = o_gf + rank
    o_sk, o_sv, o_ga = o_sq + sbw, o_sq + 2 * sbw, o_sq + 3 * sbw
    rpad = -(-rank // LANES) * LANES
    n_p = x_prompt.shape[0]
    y_p, y_s = x_prompt, x_sample
    outs = [[] for _ in range(6)]
    for l in range(depth):
        wi = w_in[l]
        p = {
            'dims': (hg, dk, dv, hs, dh), 'n_experts': n_experts,
            'w_g4': wi[:, :o_gf].astype(BF16),
            'w_gf': jnp.pad(wi[:, o_gf:o_sq], ((0, 0), (0, rpad - rank))).astype(BF16),
            'w_gla_f': jnp.pad(w_gla_f[l], ((0, rpad - rank), (0, 0))).astype(BF16),
            'b_gla_f': b_gla_f[l],
            'w_sq': wi[:, o_sq:o_sk].astype(BF16), 'w_sk': wi[:, o_sk:o_sv].astype(BF16),
            'w_sv': wi[:, o_sv:o_ga].astype(BF16), 'w_gates': wi[:, o_ga:].astype(BF16),
            'gla_norm_g': gla_norm_g[l],
            'w_gla_out': w_gla_out[l].astype(BF16), 'w_sb_out': w_sb_out[l].astype(BF16),
            'w_o': w_o[l].astype(BF16), 'ln1_g': ln1_g[l], 'ln1_b': ln1_b[l],
            'w_router': w_router[l].astype(BF16), 'b_router': b_router[l],
            'ln2_g': ln2_g[l], 'ln2_b': ln2_b[l],
            'experts': tuple(_split_gate_up(w_gate_up[l])) + (
                b_gate_up[l][:, None, 0::2], b_gate_up[l][:, None, 1::2],
                w_down[l].astype(BF16), b_down[l][:, None, :]),
        }
        mod = _ada(jnp.concatenate([c_prompt, c_sample], axis=0), w_ada[l], b_ada[l])
        y_s, sf, kn, vn = _layer(y_s, mod[n_p:], state_gla[l], (cache_sb_k, cache_sb_v, l), p, alpha)
        outs[3].append(sf), outs[4].append(kn), outs[5].append(vn)
        y_p, sf, kn, vn = _layer(y_p, mod[:n_p], None, None, p, alpha)
        outs[0].append(sf), outs[1].append(kn), outs[2].append(vn)
    return (y_p, y_s) + tuple(jnp.stack(o) for o in outs)
```

```python
import functools

import jax
import jax.numpy as jnp
from jax import lax
from jax.experimental import pallas as pl
from jax.experimental.pallas import tpu as pltpu

F32 = jnp.float32
BF16 = jnp.bfloat16

GLA_CHUNK = 64
GLA_SUB = 16
GLA_TAU = 16.0
TOP_K = 4
SWIGLU_LIMIT = 7.0
SWIGLU_ALPHA = 1.702
LN_EPS = 1e-5
RMS_EPS = 1e-6
LANES = 128
VMEM_LIMIT_BYTES = 56 * 1024 * 1024


def _cparams(*sem):
    return pltpu.CompilerParams(dimension_semantics=sem, vmem_limit_bytes=VMEM_LIMIT_BYTES)


def _tile(n, want):
    t = min(want, n)
    while n % t:
        t //= 2
    return t


def _log_sigmoid(x):
    return jnp.minimum(x, 0.0) - jnp.log1p(jnp.exp(-jnp.abs(x)))


def _layernorm(u, g, b):
    mu = jnp.mean(u, axis=-1, keepdims=True)
    d = u - mu
    var = jnp.mean(d * d, axis=-1, keepdims=True)
    return d * lax.rsqrt(var + LN_EPS) * g + b


def _mod_operand(mod, n_tok, t_per_batch, tm):
    b, d = mod.shape
    if t_per_batch % tm == 0:
        per = t_per_batch // tm
        return mod.reshape(b, 1, d), pl.BlockSpec((None, 1, d), lambda i, *_: (i // per, 0, 0))
    rows = jnp.repeat(mod, t_per_batch, axis=0)
    return rows, pl.BlockSpec((tm, d), lambda i, *_: (i, 0))


def _ada_kernel(c_ref, w_ref, b_ref, o_ref):
    c = c_ref[...]
    s = c * jax.nn.sigmoid(c)
    o_ref[...] = jnp.dot(s.astype(BF16), w_ref[...].astype(BF16),
                         preferred_element_type=F32) + b_ref[...]


def _ada(c, w, b):
    bc, d = c.shape
    n = w.shape[1]
    tn = min(1024, n)
    return pl.pallas_call(
        _ada_kernel,
        grid=(n // tn,),
        in_specs=[pl.BlockSpec((bc, d), lambda j: (0, 0)),
                  pl.BlockSpec((d, tn), lambda j: (0, j)),
                  pl.BlockSpec((1, tn), lambda j: (0, j))],
        out_specs=pl.BlockSpec((bc, tn), lambda j: (0, j)),
        out_shape=jax.ShapeDtypeStruct((bc, n), F32),
        compiler_params=_cparams("parallel"),
        name="ada",
    )(c, w, b.reshape(1, n))


def _modmm_kernel(x_ref, sc_ref, sh_ref, w_ref, o_ref, hb_ref):
    @pl.when(pl.program_id(1) == 0)
    def _():
        hb_ref[...] = (x_ref[...] * (1.0 + sc_ref[...]) + sh_ref[...]).astype(BF16)

    o_ref[...] = jnp.dot(hb_ref[...], w_ref[...], preferred_element_type=F32).astype(o_ref.dtype)


def _modmm(x2, sc, sh, w, t_per_batch, out_dtype=F32, name="modmm"):
    n, d = x2.shape
    c = w.shape[1]
    tm = _tile(n, 1024)
    tn = _tile(c, 1024)
    sc_op, sc_spec = _mod_operand(sc, n, t_per_batch, tm)
    sh_op, sh_spec = _mod_operand(sh, n, t_per_batch, tm)
    return pl.pallas_call(
        _modmm_kernel,
        grid=(n // tm, c // tn),
        in_specs=[pl.BlockSpec((tm, d), lambda i, j: (i, 0)), sc_spec, sh_spec,
                  pl.BlockSpec((d, tn), lambda i, j: (0, j))],
        out_specs=pl.BlockSpec((tm, tn), lambda i, j: (i, j)),
        out_shape=jax.ShapeDtypeStruct((n, c), out_dtype),
        scratch_shapes=[pltpu.VMEM((tm, d), BF16)],
        compiler_params=_cparams("parallel", "arbitrary"),
        name=name,
    )(x2, sc_op, sh_op, w)


def _loga_kernel(x_ref, sc_ref, sh_ref, wgf_ref, wf_ref, bf_ref, o_ref):
    h = (x_ref[...] * (1.0 + sc_ref[...]) + sh_ref[...]).astype(BF16)
    gf = jnp.dot(h, wgf_ref[...], preferred_element_type=F32)
    z = jnp.dot(gf.astype(BF16), wf_ref[...], preferred_element_type=F32) + bf_ref[...]
    o_ref[...] = _log_sigmoid(z) / GLA_TAU


def _loga(x2, sc, sh, w_gf, w_f, b_f, t_per_batch):
    n, d = x2.shape
    rp = w_gf.shape[1]
    qk = w_f.shape[1]
    tm = min(512, n)
    sc_op, sc_spec = _mod_operand(sc, n, t_per_batch, tm)
    sh_op, sh_spec = _mod_operand(sh, n, t_per_batch, tm)
    return pl.pallas_call(
        _loga_kernel,
        grid=(n // tm,),
        in_specs=[pl.BlockSpec((tm, d), lambda i: (i, 0)), sc_spec, sh_spec,
                  pl.BlockSpec((d, rp), lambda i: (0, 0)),
                  pl.BlockSpec((rp, qk), lambda i: (0, 0)),
                  pl.BlockSpec((1, qk), lambda i: (0, 0))],
        out_specs=pl.BlockSpec((tm, qk), lambda i: (i, 0)),
        out_shape=jax.ShapeDtypeStruct((n, qk), F32),
        compiler_params=_cparams("parallel"),
        name="loga",
    )(x2, sc_op, sh_op, w_gf, w_f, b_f.reshape(1, qk))


def _gla_kernel(*refs, chunk, n_chunks, has_s0, q_scale, hp):
    if has_s0:
        q_ref, k_ref, v_ref, gg_ref, la_ref, gn_ref, s0_ref, y_ref, sfin_ref, st_ref = refs
    else:
        q_ref, k_ref, v_ref, gg_ref, la_ref, gn_ref, y_ref, sfin_ref, st_ref = refs
    c_idx = pl.program_id(2)
    dk = q_ref.shape[-1] // hp
    dv = v_ref.shape[-1] // hp
    n_sub = chunk // GLA_SUB

    @pl.when(c_idx == 0)
    def _():
        for hh in range(hp):
            if has_s0:
                st_ref[hh] = s0_ref[hh].T
            else:
                st_ref[hh] = jnp.zeros((dv, dk), F32)

    row_c = lax.broadcasted_iota(jnp.int32, (chunk, chunk), 0)
    col_c = lax.broadcasted_iota(jnp.int32, (chunk, chunk), 1)
    tri_incl = (row_c >= col_c).astype(F32)
    row_s = lax.broadcasted_iota(jnp.int32, (GLA_SUB, GLA_SUB), 0)
    col_s = lax.broadcasted_iota(jnp.int32, (GLA_SUB, GLA_SUB), 1)
    gnorm = gn_ref[...]

    def head_chunk(r0, hh):
        kcols = slice(hh * dk, (hh + 1) * dk)
        vcols = slice(hh * dv, (hh + 1) * dv)
        q = q_ref[pl.ds(r0, chunk), kcols] * q_scale
        k = k_ref[pl.ds(r0, chunk), kcols]
        v = v_ref[pl.ds(r0, chunk), vcols]
        la = la_ref[pl.ds(r0, chunk), kcols]
        vb = v.astype(BF16)
        cum = jnp.dot(tri_incl, la, preferred_element_type=F32, precision=lax.Precision.HIGHEST)
        st = st_ref[hh]
        o = lax.dot_general((q * jnp.exp(cum)).astype(BF16), st.astype(BF16),
                            (((1,), (1,)), ((), ())), preferred_element_type=F32)
        o_rows = []
        for i in range(n_sub):
            lo = i * GLA_SUB
            qi, ki, ci_ = q[lo:lo + GLA_SUB], k[lo:lo + GLA_SUB], cum[lo:lo + GLA_SUB]
            sd = jnp.zeros((GLA_SUB, GLA_SUB), F32)
            for j in range(GLA_SUB):
                e = jnp.exp(jnp.minimum(ci_ - ci_[j:j + 1], 0.0))
                col = jnp.sum(qi * ki[j:j + 1] * e, axis=1, keepdims=True)
                sd = jnp.where(col_s == j, col, sd)
            sd = jnp.where(row_s >= col_s, sd, 0.0)
            oi = jnp.dot(sd.astype(BF16), vb[lo:lo + GLA_SUB], preferred_element_type=F32)
            if i > 0:
                ref_row = cum[lo - 1:lo]
                qd = (qi * jnp.exp(ci_ - ref_row)).astype(BF16)
                kd = (k[:lo] * jnp.exp(ref_row - cum[:lo])).astype(BF16)
                so = lax.dot_general(qd, kd, (((1,), (1,)), ((), ())), preferred_element_type=F32)
                oi = oi + jnp.dot(so.astype(BF16), vb[:lo], preferred_element_type=F32)
            o_rows.append(oi)
        o = o + (jnp.concatenate(o_rows, axis=0) if n_sub > 1 else o_rows[0])
        last = cum[chunk - 1:chunk]
        kdec = (k * jnp.exp(last - cum)).astype(BF16)
        st_ref[hh] = st * jnp.exp(last) + lax.dot_general(
            vb, kdec, (((0,), (0,)), ((), ())), preferred_element_type=F32)
        gg = gg_ref[pl.ds(r0, chunk), vcols]
        rms = lax.rsqrt(jnp.mean(o * o, axis=-1, keepdims=True) + RMS_EPS)
        y_ref[pl.ds(r0, chunk), vcols] = (o * rms * gnorm * (gg * jax.nn.sigmoid(gg))).astype(y_ref.dtype)

    def chunk_body(ci, carry):
        r0 = pl.multiple_of(ci * chunk, chunk)
        for hh in range(hp):
            head_chunk(r0, hh)
        return carry

    lax.fori_loop(0, n_chunks, chunk_body, 0)

    @pl.when(c_idx == pl.num_programs(2) - 1)
    def _():
        for hh in range(hp):
            sfin_ref[hh] = st_ref[hh].T


def _gla(g4, la, gnorm, s0, n_heads, dk, dv):
    b, t, _ = g4.shape
    chunk = min(GLA_CHUNK, t)
    tc = min(512, t)
    n_chunks = tc // chunk
    has_s0 = s0 is not None
    hp = 2 if n_heads % 2 == 0 else 1
    wk, wv = hp * dk, hp * dv
    assert (n_heads * dk) % wk == 0 and (2 * n_heads * dk) % wv == 0
    kq = (n_heads * dk) // wk
    kv = (2 * n_heads * dk) // wv
    kg = kv + n_heads // hp
    in_specs = [pl.BlockSpec((None, tc, wk), lambda bi, h, c: (bi, c, h)),
                pl.BlockSpec((None, tc, wk), lambda bi, h, c: (bi, c, kq + h)),
                pl.BlockSpec((None, tc, wv), lambda bi, h, c: (bi, c, kv + h)),
                pl.BlockSpec((None, tc, wv), lambda bi, h, c: (bi, c, kg + h)),
                pl.BlockSpec((None, tc, wk), lambda bi, h, c: (bi, c, h)),
                pl.BlockSpec((1, dv), lambda bi, h, c: (0, 0))]
    args = [g4, g4, g4, g4, la, gnorm.reshape(1, dv)]
    if has_s0:
        in_specs.append(pl.BlockSpec((None, hp, dk, dv), lambda bi, h, c: (bi, h, 0, 0)))
        args.append(s0)
    kern = functools.partial(_gla_kernel, chunk=chunk, n_chunks=n_chunks, has_s0=has_s0,
                             q_scale=dk ** -0.5, hp=hp)
    return pl.pallas_call(
        kern,
        grid=(b, n_heads // hp, t // tc),
        in_specs=in_specs,
        out_specs=[pl.BlockSpec((None, tc, wv), lambda bi, h, c: (bi, c, h)),
                   pl.BlockSpec((None, hp, dk, dv), lambda bi, h, c: (bi, h, 0, 0))],
        out_shape=[jax.ShapeDtypeStruct((b, t, n_heads * dv), BF16),
                   jax.ShapeDtypeStruct((b, n_heads, dk, dv), F32)],
        scratch_shapes=[pltpu.VMEM((hp, dv, dk), F32)],
        compiler_params=_cparams("parallel", "parallel", "arbitrary"),
        name="gla",
    )(*args)


SB_GROUP = 4
SB_DEAD_LOG2 = -151.0
LOG2_E = 1.4426950408889634


def _sb_split_dot(log_1m, u, u_first):
    x = log_1m.astype(BF16)
    if u_first:
        return jnp.dot(u, x, preferred_element_type=F32)
    return jnp.dot(x, u, preferred_element_type=F32)


def _sb_log2_terms(z2):
    neg = jnp.minimum(z2, 0.0)
    log_beta = neg - jnp.log2(1.0 + jnp.exp2((neg + neg) - z2))
    return log_beta, log_beta - z2


def _sb_scores(qb, kb, u, masked):
    tq, tk = qb.shape[0], kb.shape[0]
    z2 = lax.dot_general(qb, kb, (((1,), (1,)), ((), ())), preferred_element_type=F32)
    log_beta, log_1m = _sb_log2_terms(z2)
    if masked:
        vis = (lax.broadcasted_iota(jnp.int32, (tq, tk), 1)
               < lax.broadcasted_iota(jnp.int32, (tq, tk), 0))
        log_1m = jnp.where(vis, log_1m, 0.0)
    later = _sb_split_dot(log_1m, u, False)
    w = jnp.exp2(log_beta + later)
    if masked:
        w = jnp.where(vis, w, 0.0)
    return w.astype(BF16), later[:, :1] + log_1m[:, :1]


def _sb_group(qb, tiles, run, acc, u):
    parts = [_sb_scores(qb, kb, u, masked) for kb, _, masked in tiles]
    for (w, total), (_, vb, _) in zip(parts, tiles):
        acc = acc + jnp.exp2(run) * jnp.dot(w, vb, preferred_element_type=F32)
        run = run + total
    return run, acc


def _sb_kernel(q_ref, kn_ref, vn_ref, u_ref, o_ref, kb_ref, vb_ref, *, tq, scale):
    i = pl.program_id(2)
    t, dh = kn_ref.shape
    ct = tq

    @pl.when(i == 0)
    def _():
        def cast(c, carry):
            r = pl.multiple_of(c * ct, ct)
            kb_ref[pl.ds(r, ct), :] = kn_ref[pl.ds(r, ct), :].astype(BF16)
            vb_ref[pl.ds(r, ct), :] = vn_ref[pl.ds(r, ct), :].astype(BF16)
            return carry

        lax.fori_loop(0, t // ct, cast, 0)

    qb = (q_ref[...] * (scale * LOG2_E)).astype(BF16)
    u = u_ref[...]

    def tile(kt, masked=False):
        r = pl.multiple_of(kt * tq, tq)
        return kb_ref[pl.ds(r, tq), :], vb_ref[pl.ds(r, tq), :], masked

    zero = (jnp.zeros((tq, 1), F32), jnp.zeros((tq, dh), F32))

    def head_pair(_):
        return _sb_group(qb, [tile(i, True), tile(i - 1)], zero[0], zero[1], u) + (i - 2,)

    def head_alone(_):
        return _sb_group(qb, [tile(i, True)], zero[0], zero[1], u) + (i - 1,)

    run, acc, top = lax.cond(i >= 1, head_pair, head_alone, None)

    def alive(r):
        return jnp.max(r) > SB_DEAD_LOG2

    def rest(run, acc, top):
        left = top + 1
        size = 1
        while size <= SB_GROUP:
            full = size == SB_GROUP
            count = lax.div(left, size) if full else lax.rem(lax.div(left, size), 2)

            def group(c, size=size, top=top):
                first = top - c[0] * size
                r, a = _sb_group(qb, [tile(first - j) for j in range(size)], c[1], c[2], u)
                return c[0] + 1, r, a

            def more(c, count=count):
                return jnp.logical_and(c[0] < count, alive(c[1]))

            _, run, acc = lax.while_loop(more, group, (jnp.int32(0), run, acc))
            top = top - count * size
            size *= 2
        return acc

    acc = lax.cond(jnp.logical_and(top >= 0, alive(run)), rest, lambda run, acc, top: acc,
                   run, acc, top)
    o_ref[...] = acc.astype(o_ref.dtype)


def _stick_breaking(q, k_new, v_new, n_heads, dh):
    b, t, _ = q.shape
    tq = min(256, t)
    u = (lax.broadcasted_iota(jnp.int32, (tq, tq), 0)
         > lax.broadcasted_iota(jnp.int32, (tq, tq), 1)).astype(BF16)
    kern = functools.partial(_sb_kernel, tq=tq, scale=dh ** -0.5)
    return pl.pallas_call(
        kern,
        grid=(b, n_heads, t // tq),
        in_specs=[pl.BlockSpec((None, tq, dh), lambda bi, h, i: (bi, i, h)),
                  pl.BlockSpec((None, t, dh), lambda bi, h, i: (bi, 0, h)),
                  pl.BlockSpec((None, t, dh), lambda bi, h, i: (bi, 0, h)),
                  pl.BlockSpec((tq, tq), lambda bi, h, i: (0, 0))],
        out_specs=pl.BlockSpec((None, tq, dh), lambda bi, h, i: (bi, i, h)),
        out_shape=jax.ShapeDtypeStruct((b, t, n_heads * dh), BF16),
        scratch_shapes=[pltpu.VMEM((t, dh), BF16), pltpu.VMEM((t, dh), BF16)],
        compiler_params=_cparams("parallel", "parallel", "arbitrary"),
        name="stick_breaking",
    )(q, k_new, v_new, u)


def _sbd_kernel(q_ref, kn_ref, vn_ref, kc_hbm, vc_hbm, u_ref, qpos_ref, o_ref,
                qbd_ref, run_ref, acc_ref, kbuf_ref, vbuf_ref, sem, *,
                layer, n_ct, n_heads, dh, t, tk, hg, scale):
    bi = pl.program_id(0)
    rows = tk * n_heads

    def fetch(j):
        r = pl.multiple_of(j * rows, rows)
        return (pltpu.make_async_copy(kc_hbm.at[layer, bi, pl.ds(r, rows), :], kbuf_ref, sem.at[0]),
                pltpu.make_async_copy(vc_hbm.at[layer, bi, pl.ds(r, rows), :], vbuf_ref, sem.at[1]))

    for cp in fetch(n_ct - 1):
        cp.start()
    qbd_ref[...] = jnp.zeros_like(qbd_ref)
    for h in range(n_heads):
        qbd_ref[h * t:(h + 1) * t, h * dh:(h + 1) * dh] = (
            q_ref[:, h * dh:(h + 1) * dh] * (scale * LOG2_E)).astype(BF16)
    run_ref[...] = jnp.zeros_like(run_ref)
    acc_ref[...] = jnp.zeros_like(acc_ref)

    def tile(get_k, get_v, masked):
        zt = None
        for g in range(n_heads // hg):
            kcat = [get_k(g * hg + j) for j in range(hg)]
            kcat = (jnp.concatenate(kcat, axis=1) if hg > 1 else kcat[0]).astype(BF16)
            part = lax.dot_general(kcat, qbd_ref[:, g * hg * dh:(g + 1) * hg * dh],
                                   (((1,), (1,)), ((), ())), preferred_element_type=F32)
            zt = part if zt is None else zt + part
        log_beta, log_1m = _sb_log2_terms(zt)
        vis = None
        if masked:
            vis = lax.broadcasted_iota(jnp.int32, zt.shape, 0) < qpos_ref[...]
            log_1m = jnp.where(vis, log_1m, 0.0)
        later = _sb_split_dot(log_1m, u_ref[...], True)
        w = jnp.exp2(log_beta + (later + run_ref[...]))
        if masked:
            w = jnp.where(vis, w, 0.0)
        run_ref[...] += later[0:1, :] + log_1m[0:1, :]
        wt = w.T
        for h in range(n_heads):
            acc_ref[:, h * dh:(h + 1) * dh] += jnp.dot(
                wt[h * t:(h + 1) * t, :].astype(BF16), get_v(h).astype(BF16),
                preferred_element_type=F32)
        return (jnp.max(run_ref[...]) > SB_DEAD_LOG2).astype(jnp.int32)

    pad = jnp.zeros((tk - t, dh), F32)
    alive = tile(lambda h: jnp.concatenate([kn_ref[:, h * dh:(h + 1) * dh], pad], axis=0),
                 lambda h: jnp.concatenate([vn_ref[:, h * dh:(h + 1) * dh], pad], axis=0), True)
    for cp in fetch(n_ct - 1):
        cp.wait()

    def cache_tile(c):
        n = c[0]
        still = tile(lambda h: kbuf_ref[pl.ds(h, tk, stride=n_heads), :],
                     lambda h: vbuf_ref[pl.ds(h, tk, stride=n_heads), :], False)

        @pl.when(jnp.logical_and(still > 0, n + 1 < n_ct))
        def _():
            for cp in fetch(n_ct - 2 - n):
                cp.start()
            for cp in fetch(n_ct - 2 - n):
                cp.wait()

        return n + 1, still

    lax.while_loop(lambda c: jnp.logical_and(c[0] < n_ct, c[1] > 0), cache_tile, (jnp.int32(0), alive))
    o_ref[...] = acc_ref[...].astype(o_ref.dtype)


def _stick_breaking_decode(q, k_new, v_new, cache_k, cache_v, layer, n_heads, dh):
    b, t, hd = q.shape
    tp = cache_k.shape[2]
    tk = min(256, tp)
    n_ct = tp // tk
    nq = n_heads * t
    nqp = -(-nq // LANES) * LANES
    hg = max(1, 2 * LANES // dh)
    if n_heads % hg:
        hg = 1
    u = (lax.broadcasted_iota(jnp.int32, (tk, tk), 1)
         > lax.broadcasted_iota(jnp.int32, (tk, tk), 0)).astype(BF16)
    col = jnp.arange(nqp, dtype=jnp.int32)
    qpos = jnp.where(col < nq, col % t, 0).reshape(1, nqp)
    ck = cache_k.reshape(cache_k.shape[0], b, tp * n_heads, dh)
    cv = cache_v.reshape(cache_v.shape[0], b, tp * n_heads, dh)
    cache_spec = pl.BlockSpec(memory_space=pl.ANY)
    new_spec = pl.BlockSpec((None, t, hd), lambda bi: (bi, 0, 0))
    kern = functools.partial(_sbd_kernel, layer=layer, n_ct=n_ct, n_heads=n_heads, dh=dh, t=t, tk=tk,
                             hg=hg, scale=dh ** -0.5)
    return pl.pallas_call(
        kern,
        grid=(b,),
        in_specs=[new_spec, new_spec, new_spec, cache_spec, cache_spec,
                  pl.BlockSpec((tk, tk), lambda bi: (0, 0)),
                  pl.BlockSpec((1, nqp), lambda bi: (0, 0))],
        out_specs=new_spec,
        out_shape=jax.ShapeDtypeStruct((b, t, hd), BF16),
        scratch_shapes=[pltpu.VMEM((nqp, hd), BF16), pltpu.VMEM((1, nqp), F32),
                        pltpu.VMEM((t, hd), F32),
                        pltpu.VMEM((tk * n_heads, dh), F32), pltpu.VMEM((tk * n_heads, dh), F32),
                        pltpu.SemaphoreType.DMA((2,))],
        compiler_params=_cparams("arbitrary"),
        name="stick_breaking_decode",
    )(q, k_new, v_new, ck, cv, u, qpos)


def _merge_kernel(ya_ref, yb_ref, wa_ref, wb_ref, ga_ref, gb_ref, o_ref):
    a = jnp.dot(ya_ref[...], wa_ref[...], preferred_element_type=F32)
    bb = jnp.dot(yb_ref[...], wb_ref[...], preferred_element_type=F32)
    o_ref[...] = (jax.nn.sigmoid(ga_ref[...]) * a + jax.nn.sigmoid(gb_ref[...]) * bb).astype(o_ref.dtype)


def _merge(ya, yb, wa, wb, gates):
    n, da = ya.shape
    db = yb.shape[1]
    d = wa.shape[1]
    tm = min(1024, n)
    tn = min(512, d)
    nj = d // tn
    return pl.pallas_call(
        _merge_kernel,
        grid=(n // tm, nj),
        in_specs=[pl.BlockSpec((tm, da), lambda i, j: (i, 0)),
                  pl.BlockSpec((tm, db), lambda i, j: (i, 0)),
                  pl.BlockSpec((da, tn), lambda i, j: (0, j)),
                  pl.BlockSpec((db, tn), lambda i, j: (0, j)),
                  pl.BlockSpec((tm, tn), lambda i, j: (i, j)),
                  pl.BlockSpec((tm, tn), lambda i, j: (i, nj + j))],
        out_specs=pl.BlockSpec((tm, tn), lambda i, j: (i, j)),
        out_shape=jax.ShapeDtypeStruct((n, d), BF16),
        compiler_params=_cparams("parallel", "arbitrary"),
        name="merge",
    )(ya, yb, wa, wb, gates, gates)


def _post_kernel(m_ref, wo_ref, x_ref, g1_ref, sc_ref, sh_ref, lg_ref, lb_ref, wr_ref, br_ref,
                 x1_ref, h2_ref, lo_ref, *, alpha):
    mix = jnp.dot(m_ref[...], wo_ref[...], preferred_element_type=F32)
    x1 = _layernorm(alpha * x_ref[...] + g1_ref[...] * mix, lg_ref[...], lb_ref[...])
    h2 = x1 * (1.0 + sc_ref[...]) + sh_ref[...]
    x1_ref[...] = x1
    h2_ref[...] = h2
    lo_ref[...] = jnp.dot(h2.astype(BF16), wr_ref[...], preferred_element_type=F32) + br_ref[...]


def _post(merged, wo, x2, g1, sc2, sh2, ln_g, ln_b, wr, br, t_per_batch, alpha):
    n, d = x2.shape
    e = wr.shape[1]
    tm = min(256, n)
    g1_op, g1_spec = _mod_operand(g1, n, t_per_batch, tm)
    sc_op, sc_spec = _mod_operand(sc2, n, t_per_batch, tm)
    sh_op, sh_spec = _mod_operand(sh2, n, t_per_batch, tm)
    row = lambda i: (i, 0)
    fix = lambda i: (0, 0)
    return pl.pallas_call(
        functools.partial(_post_kernel, alpha=alpha),
        grid=(n // tm,),
        in_specs=[pl.BlockSpec((tm, d), row), pl.BlockSpec((d, d), fix), pl.BlockSpec((tm, d), row),
                  g1_spec, sc_spec, sh_spec,
                  pl.BlockSpec((1, d), fix), pl.BlockSpec((1, d), fix),
                  pl.BlockSpec((d, e), fix), pl.BlockSpec((1, e), fix)],
        out_specs=[pl.BlockSpec((tm, d), row), pl.BlockSpec((tm, d), row), pl.BlockSpec((tm, e), row)],
        out_shape=[jax.ShapeDtypeStruct((n, d), F32), jax.ShapeDtypeStruct((n, d), F32),
                   jax.ShapeDtypeStruct((n, e), F32)],
        compiler_params=_cparams("parallel"),
        name="post",
    )(merged, wo, x2, g1_op, sc_op, sh_op, ln_g.reshape(1, d), ln_b.reshape(1, d), wr,
      br.reshape(1, e))


def _cols_to_lanes(cols):
    tm = cols[0].shape[0]
    lane = lax.broadcasted_iota(jnp.int32, (tm, len(cols)), 1)
    out = jnp.broadcast_to(cols[0], (tm, len(cols)))
    for kk in range(1, len(cols)):
        out = jnp.where(lane == kk, cols[kk], out)
    return out


def _route_kernel(lo_ref, idx_ref, w_ref, rank_ref, cnt_ref, carry_ref):
    @pl.when(pl.program_id(0) == 0)
    def _():
        carry_ref[...] = jnp.zeros_like(carry_ref)

    lg = lo_ref[...]
    tm, e = lg.shape
    lane = lax.broadcasted_iota(jnp.int32, (tm, e), 1).astype(F32)
    vals, idxs = [], []
    chosen = jnp.zeros((tm, e), F32)
    for _ in range(TOP_K):
        m = jnp.max(lg, axis=-1, keepdims=True)
        ik = jnp.min(jnp.where(lg == m, lane, float(e)), axis=-1, keepdims=True)
        sel = lane == ik
        vals.append(m)
        idxs.append(ik)
        chosen = jnp.where(sel, 1.0, chosen)
        lg = jnp.where(sel, -jnp.inf, lg)
    ex = [jnp.exp(v - vals[0]) for v in vals]
    tot = ex[0]
    for t_ in ex[1:]:
        tot = tot + t_
    before = (lax.broadcasted_iota(jnp.int32, (tm, tm), 1)
              < lax.broadcasted_iota(jnp.int32, (tm, tm), 0)).astype(BF16)
    pos = jnp.dot(before, chosen.astype(BF16), preferred_element_type=F32) + carry_ref[...]
    ranks = [jnp.sum(jnp.where(lane == ik, pos, 0.0), axis=-1, keepdims=True) for ik in idxs]
    idx_ref[...] = _cols_to_lanes(idxs).astype(jnp.int32)
    w_ref[...] = _cols_to_lanes([x / tot for x in ex])
    rank_ref[...] = _cols_to_lanes(ranks).astype(jnp.int32)
    carry_ref[...] = carry_ref[...] + jnp.sum(chosen, axis=0, keepdims=True)
    cnt_ref[...] = carry_ref[...].astype(jnp.int32)


def _route(logits):
    n, e = logits.shape
    tm = min(256, n)
    row = lambda i: (i, 0)
    return pl.pallas_call(
        _route_kernel,
        grid=(n // tm,),
        in_specs=[pl.BlockSpec((tm, e), row)],
        out_specs=[pl.BlockSpec((tm, TOP_K), row), pl.BlockSpec((tm, TOP_K), row),
                   pl.BlockSpec((tm, TOP_K), row), pl.BlockSpec((1, e), lambda i: (0, 0))],
        out_shape=[jax.ShapeDtypeStruct((n, TOP_K), jnp.int32), jax.ShapeDtypeStruct((n, TOP_K), F32),
                   jax.ShapeDtypeStruct((n, TOP_K), jnp.int32), jax.ShapeDtypeStruct((1, e), jnp.int32)],
        scratch_shapes=[pltpu.VMEM((1, e), F32)],
        compiler_params=_cparams("arbitrary"),
        name="route",
    )(logits)


def _dest_kernel(idx_ref, rank_ref, start_ref, o_ref):
    idx = idx_ref[...]
    tm = idx.shape[0]
    e = start_ref.shape[1]
    lane = lax.broadcasted_iota(jnp.int32, (tm, e), 1)
    start = start_ref[...].astype(F32)
    cols = [jnp.sum(jnp.where(lane == idx[:, kk:kk + 1], start, 0.0), axis=-1, keepdims=True)
            for kk in range(TOP_K)]
    o_ref[...] = _cols_to_lanes(cols).astype(jnp.int32) + rank_ref[...]


def _dest(idx, rank, group_start):
    n = idx.shape[0]
    e = group_start.shape[1]
    tm = min(256, n)
    row = lambda i: (i, 0)
    return pl.pallas_call(
        _dest_kernel,
        grid=(n // tm,),
        in_specs=[pl.BlockSpec((tm, TOP_K), row), pl.BlockSpec((tm, TOP_K), row),
                  pl.BlockSpec((1, e), lambda i: (0, 0))],
        out_specs=pl.BlockSpec((tm, TOP_K), row),
        out_shape=jax.ShapeDtypeStruct((n, TOP_K), jnp.int32),
        compiler_params=_cparams("parallel"),
        name="dest",
    )(idx, rank, group_start)


def _split_gate_up_kernel(w_ref, p_ref, g_ref, u_ref):
    x = w_ref[...].astype(BF16)
    p = p_ref[...]
    width = p.shape[0]
    half = width // 2
    for g in range(x.shape[1] // width):
        res = jnp.dot(x[:, g * width:(g + 1) * width], p, preferred_element_type=F32)
        g_ref[:, g * half:(g + 1) * half] = res[:, :half].astype(BF16)
        u_ref[:, g * half:(g + 1) * half] = res[:, half:].astype(BF16)


def _split_gate_up(w):
    e, d, f2 = w.shape
    width = 2 * LANES
    tr = min(1024, d)
    tc = min(1024, f2)
    r = lax.broadcasted_iota(jnp.int32, (width, width), 0)
    c = lax.broadcasted_iota(jnp.int32, (width, width), 1)
    perm = jnp.where(c < LANES, r == 2 * c, r == 2 * (c - LANES) + 1).astype(BF16)
    return pl.pallas_call(
        _split_gate_up_kernel,
        grid=(e, d // tr, f2 // tc),
        in_specs=[pl.BlockSpec((None, tr, tc), lambda ei, ri, ci: (ei, ri, ci)),
                  pl.BlockSpec((width, width), lambda ei, ri, ci: (0, 0))],
        out_specs=[pl.BlockSpec((None, tr, tc // 2), lambda ei, ri, ci: (ei, ri, ci)),
                   pl.BlockSpec((None, tr, tc // 2), lambda ei, ri, ci: (ei, ri, ci))],
        out_shape=[jax.ShapeDtypeStruct((e, d, f2 // 2), BF16),
                   jax.ShapeDtypeStruct((e, d, f2 // 2), BF16)],
        compiler_params=_cparams("parallel", "parallel", "parallel"),
        name="split_gate_up",
    )(w, perm)


def _row_copy(src_ref, src_row, dst_ref, dst_row, sem):
    return pltpu.make_async_copy(src_ref.at[pl.ds(src_row, 1), :], dst_ref.at[pl.ds(dst_row, 1), :], sem)


def _dispatch_kernel(dest_ref, h_ref, xs_in_ref, xs_ref, sem):
    del xs_in_ref
    tm = h_ref.shape[0]

    def issue(t, c):
        for kk in range(TOP_K):
            _row_copy(h_ref, t, xs_ref, dest_ref[t * TOP_K + kk], sem).start()
        return c

    lax.fori_loop(0, tm, issue, 0)

    def drain(t, c):
        for kk in range(TOP_K):
            _row_copy(h_ref, 0, xs_ref, 0, sem).wait()
        return c

    lax.fori_loop(0, tm, drain, 0)


def _dispatch(dest_flat, h2, n_rows):
    n, d = h2.shape
    tm = min(256, n)
    xs0 = jnp.zeros((n_rows, d), F32)
    return pl.pallas_call(
        _dispatch_kernel,
        grid=(n // tm,),
        in_specs=[pl.BlockSpec((tm * TOP_K,), lambda i: (i,), memory_space=pltpu.SMEM),
                  pl.BlockSpec((tm, d), lambda i: (i, 0)),
                  pl.BlockSpec(memory_space=pl.ANY)],
        out_specs=pl.BlockSpec(memory_space=pl.ANY),
        out_shape=jax.ShapeDtypeStruct((n_rows, d), F32),
        scratch_shapes=[pltpu.SemaphoreType.DMA(())],
        input_output_aliases={2: 0},
        compiler_params=_cparams("arbitrary"),
        name="dispatch",
    )(dest_flat, h2, xs0)


EXPERT_ROW_SPLIT = 1


def _experts_kernel(be_ref, nu_ref, xs_ref, wg_ref, wu_ref, bg_ref, bu_ref, wd_ref, bd_ref,
                    y_ref, xb_ref):
    del be_ref
    j = pl.program_id(0)
    f = pl.program_id(1)

    @pl.when(j < nu_ref[0])
    def _():
        @pl.when(f == 0)
        def _():
            xb_ref[...] = xs_ref[...].astype(BF16)
            y_ref[...] = jnp.broadcast_to(bd_ref[...], y_ref.shape)

        rows = xb_ref.shape[0] // EXPERT_ROW_SPLIT
        for r in range(EXPERT_ROW_SPLIT):
            sl = slice(r * rows, (r + 1) * rows)
            xb = xb_ref[sl, :]
            gate = jnp.dot(xb, wg_ref[...], preferred_element_type=F32) + bg_ref[...]
            up = jnp.dot(xb, wu_ref[...], preferred_element_type=F32) + bu_ref[...]
            gate = jnp.minimum(gate, SWIGLU_LIMIT)
            up = jnp.clip(up, -SWIGLU_LIMIT, SWIGLU_LIMIT)
            act = (up + 1.0) * (gate * jax.nn.sigmoid(SWIGLU_ALPHA * gate))
            y_ref[sl, :] += jnp.dot(act.astype(BF16), wd_ref[...], preferred_element_type=F32)

    @pl.when(jnp.logical_and(j >= nu_ref[0], f == 0))
    def _():
        y_ref[...] = jnp.zeros_like(y_ref)


def _experts(xs, block_e, n_used, wg, wu, bg, bu, wd, bd, rb):
    n_rows, d = xs.shape
    e, _, dff = wg.shape
    nb = n_rows // rb
    tf = min(512, dff)
    nf = dff // tf

    def live(j, nu):
        return jnp.minimum(j, nu[0] - 1)

    def fidx(j, f, nu):
        return jnp.where(j < nu[0], f, nf - 1)

    grid_spec = pltpu.PrefetchScalarGridSpec(
        num_scalar_prefetch=2,
        grid=(nb, nf),
        in_specs=[
            pl.BlockSpec((rb, d), lambda j, f, be, nu: (live(j, nu), 0)),
            pl.BlockSpec((None, d, tf), lambda j, f, be, nu: (be[live(j, nu)], 0, fidx(j, f, nu))),
            pl.BlockSpec((None, d, tf), lambda j, f, be, nu: (be[live(j, nu)], 0, fidx(j, f, nu))),
            pl.BlockSpec((None, 1, tf), lambda j, f, be, nu: (be[live(j, nu)], 0, fidx(j, f, nu))),
            pl.BlockSpec((None, 1, tf), lambda j, f, be, nu: (be[live(j, nu)], 0, fidx(j, f, nu))),
            pl.BlockSpec((None, tf, d), lambda j, f, be, nu: (be[live(j, nu)], fidx(j, f, nu), 0)),
            pl.BlockSpec((None, 1, d), lambda j, f, be, nu: (be[live(j, nu)], 0, 0)),
        ],
        out_specs=pl.BlockSpec((rb, d), lambda j, f, be, nu: (j, 0)),
        scratch_shapes=[pltpu.VMEM((rb, d), BF16)],
    )
    return pl.pallas_call(
        _experts_kernel,
        grid_spec=grid_spec,
        out_shape=jax.ShapeDtypeStruct((n_rows, d), F32),
        compiler_params=_cparams("arbitrary", "arbitrary"),
        name="experts",
    )(block_e, n_used, xs, wg, wu, bg, bu, wd, bd)


def _combine_kernel(dest_ref, y_ref, w_ref, x1_ref, g2_ref, lg_ref, lb_ref, o_ref, buf_ref, sem,
                    *, alpha):
    tm = x1_ref.shape[0]

    def issue(t, c):
        for kk in range(TOP_K):
            _row_copy(y_ref, dest_ref[t * TOP_K + kk], buf_ref.at[kk], t, sem).start()
        return c

    lax.fori_loop(0, tm, issue, 0)

    def drain(t, c):
        for kk in range(TOP_K):
            _row_copy(y_ref, 0, buf_ref.at[kk], 0, sem).wait()
        return c

    lax.fori_loop(0, tm, drain, 0)
    w = w_ref[...]
    ffn = w[:, 0:1] * buf_ref[0]
    for kk in range(1, TOP_K):
        ffn = ffn + w[:, kk:kk + 1] * buf_ref[kk]
    o_ref[...] = _layernorm(alpha * x1_ref[...] + g2_ref[...] * ffn, lg_ref[...], lb_ref[...])


def _combine(dest_flat, y, top_w, x1, g2, ln_g, ln_b, t_per_batch, alpha):
    n, d = x1.shape
    tm = min(256, n)
    g2_op, g2_spec = _mod_operand(g2, n, t_per_batch, tm)
    row = lambda i: (i, 0)
    fix = lambda i: (0, 0)
    return pl.pallas_call(
        functools.partial(_combine_kernel, alpha=alpha),
        grid=(n // tm,),
        in_specs=[pl.BlockSpec((tm * TOP_K,), lambda i: (i,), memory_space=pltpu.SMEM),
                  pl.BlockSpec(memory_space=pl.ANY),
                  pl.BlockSpec((tm, TOP_K), row), pl.BlockSpec((tm, d), row), g2_spec,
                  pl.BlockSpec((1, d), fix), pl.BlockSpec((1, d), fix)],
        out_specs=pl.BlockSpec((tm, d), row),
        out_shape=jax.ShapeDtypeStruct((n, d), F32),
        scratch_shapes=[pltpu.VMEM((TOP_K, tm, d), F32), pltpu.SemaphoreType.DMA(())],
        compiler_params=_cparams("arbitrary"),
        name="combine",
    )(dest_flat, y, top_w, x1, g2_op, ln_g.reshape(1, d), ln_b.reshape(1, d))


def _moe(h2, logits, x1, g2, ln_g, ln_b, ew, t_per_batch, alpha, rb):
    n, d = h2.shape
    e = logits.shape[1]
    idx, top_w, rank, counts = _route(logits)
    padded = (counts + rb - 1) // rb * rb
    group_end = jnp.cumsum(padded, axis=1)
    dest = _dest(idx, rank, group_end - padded)
    dest_flat = dest.reshape(n * TOP_K)
    nb = -(-(n * TOP_K) // rb) + e
    block_e = jnp.minimum(jnp.searchsorted(group_end[0], jnp.arange(nb, dtype=jnp.int32) * rb,
                                           side='right'), e - 1).astype(jnp.int32)
    n_used = (group_end[0, -1:] // rb).astype(jnp.int32)
    xs = _dispatch(dest_flat, h2, nb * rb)
    y = _experts(xs, block_e, n_used, *ew, rb=rb)
    return _combine(dest_flat, y, top_w, x1, g2, ln_g, ln_b, t_per_batch, alpha)


def _layer(x, mod, s0, cache, p, alpha):
    b, t, d = x.shape
    n = b * t
    sh1, sc1, g1, sh2, sc2, g2 = [mod[:, i * d:(i + 1) * d] for i in range(6)]
    x2 = x.reshape(n, d)
    hg, dk, dv, hs, dh = p['dims']
    g4 = _modmm(x2, sc1, sh1, p['w_g4'], t, name="proj_gla")
    la = _loga(x2, sc1, sh1, p['w_gf'], p['w_gla_f'], p['b_gla_f'], t)
    sq = _modmm(x2, sc1, sh1, p['w_sq'], t, name="proj_sq")
    sk = _modmm(x2, sc1, sh1, p['w_sk'], t, name="proj_sk")
    sv = _modmm(x2, sc1, sh1, p['w_sv'], t, name="proj_sv")
    gates = _modmm(x2, sc1, sh1, p['w_gates'], t, name="proj_gates")
    y_gla, s_fin = _gla(g4.reshape(b, t, -1), la.reshape(b, t, -1), p['gla_norm_g'], s0, hg, dk, dv)
    q3, k3, v3 = sq.reshape(b, t, -1), sk.reshape(b, t, -1), sv.reshape(b, t, -1)
    if cache is None:
        o_sb = _stick_breaking(q3, k3, v3, hs, dh)
    else:
        o_sb = _stick_breaking_decode(q3, k3, v3, cache[0], cache[1], cache[2], hs, dh)
    merged = _merge(y_gla.reshape(n, -1), o_sb.reshape(n, -1), p['w_gla_out'], p['w_sb_out'], gates)
    x1, h2, logits = _post(merged, p['w_o'], x2, g1, sc2, sh2, p['ln1_g'], p['ln1_b'],
                           p['w_router'], p['b_router'], t, alpha)
    rb = 512 if n * TOP_K >= 512 * p['n_experts'] * 4 else 128
    rb = min(rb, n)
    x_out = _moe(h2, logits, x1, g2, p['ln2_g'], p['ln2_b'], p['experts'], t, alpha, rb)
    return (x_out.reshape(b, t, d), s_fin, sk.reshape(b, t, hs, dh), sv.reshape(b, t, hs, dh))


def kernel(x_prompt, x_sample, c_prompt, c_sample, state_gla, cache_sb_k, cache_sb_v, w_ada, b_ada, w_in, w_gla_f, b_gla_f, gla_norm_g, w_gla_out, w_sb_out, w_o, ln1_g, ln1_b, w_router, b_router, w_gate_up, b_gate_up, w_down, b_down, ln2_g, ln2_b):
    depth = w_ada.shape[0]
    d = x_prompt.shape[-1]
    hg, dk, dv = state_gla.shape[2:]
    hs, dh = cache_sb_k.shape[3:]
    rank = w_gla_f.shape[1]
    n_experts = w_router.shape[-1]
    alpha = (2.0 * depth) ** 0.25
    qk, gv, sbw = hg * dk, hg * dv, hs * dh
    o_gf = 2 * qk + 2 * gv
    o_sq = o_gf + rank
    o_sk, o_sv, o_ga = o_sq + sbw, o_sq + 2 * sbw, o_sq + 3 * sbw
    rpad = -(-rank // LANES) * LANES
    n_p = x_prompt.shape[0]
    y_p, y_s = x_prompt, x_sample
    outs = [[] for _ in range(6)]
    for l in range(depth):
        wi = w_in[l]
        p = {
            'dims': (hg, dk, dv, hs, dh), 'n_experts': n_experts,
            'w_g4': wi[:, :o_gf].astype(BF16),
            'w_gf': jnp.pad(wi[:, o_gf:o_sq], ((0, 0), (0, rpad - rank))).astype(BF16),
            'w_gla_f': jnp.pad(w_gla_f[l], ((0, rpad - rank), (0, 0))).astype(BF16),
            'b_gla_f': b_gla_f[l],
            'w_sq': wi[:, o_sq:o_sk].astype(BF16), 'w_sk': wi[:, o_sk:o_sv].astype(BF16),
            'w_sv': wi[:, o_sv:o_ga].astype(BF16), 'w_gates': wi[:, o_ga:].astype(BF16),
            'gla_norm_g': gla_norm_g[l],
            'w_gla_out': w_gla_out[l].astype(BF16), 'w_sb_out': w_sb_out[l].astype(BF16),
            'w_o': w_o[l].astype(BF16), 'ln1_g': ln1_g[l], 'ln1_b': ln1_b[l],
            'w_router': w_router[l].astype(BF16), 'b_router': b_router[l],
            'ln2_g': ln2_g[l], 'ln2_b': ln2_b[l],
            'experts': tuple(_split_gate_up(w_gate_up[l])) + (
                b_gate_up[l][:, None, 0::2], b_gate_up[l][:, None, 1::2],
                w_down[l].astype(BF16), b_down[l][:, None, :]),
        }
        mod = _ada(jnp.concatenate([c_prompt, c_sample], axis=0), w_ada[l], b_ada[l])
        y_s, sf, kn, vn = _layer(y_s, mod[n_p:], state_gla[l], (cache_sb_k, cache_sb_v, l), p, alpha)
        outs[3].append(sf), outs[4].append(kn), outs[5].append(vn)
        y_p, sf, kn, vn = _layer(y_p, mod[:n_p], None, None, p, alpha)
        outs[0].append(sf), outs[1].append(kn), outs[2].append(vn)
    return (y_p, y_s) + tuple(jnp.stack(o) for o in outs)
```

```python
import functools

import jax
import jax.numpy as jnp
from jax import lax
from jax.experimental import pallas as pl
from jax.experimental.pallas import tpu as pltpu

F32 = jnp.float32
BF16 = jnp.bfloat16

GLA_CHUNK = 64
GLA_SUB = 16
GLA_TAU = 16.0
TOP_K = 4
SWIGLU_LIMIT = 7.0
SWIGLU_ALPHA = 1.702
LN_EPS = 1e-5
RMS_EPS = 1e-6
LANES = 128
SUBLANES = 8
VMEM_LIMIT_BYTES = 56 * 1024 * 1024


def _cparams(*sem):
    return pltpu.CompilerParams(dimension_semantics=sem, vmem_limit_bytes=VMEM_LIMIT_BYTES)


def _tile(n, want):
    t = min(want, n)
    while n % t:
        t //= 2
    return t


def _log_sigmoid(x):
    return jnp.minimum(x, 0.0) - jnp.log1p(jnp.exp(-jnp.abs(x)))


def _layernorm(u, g, b):
    mu = jnp.mean(u, axis=-1, keepdims=True)
    d = u - mu
    var = jnp.mean(d * d, axis=-1, keepdims=True)
    return d * lax.rsqrt(var + LN_EPS) * g + b


def _mod_operand(mod, n_tok, t_per_batch, tm):
    b, d = mod.shape
    if t_per_batch % tm == 0:
        per = t_per_batch // tm
        return mod.reshape(b, 1, d), pl.BlockSpec((None, 1, d), lambda i, *_: (i // per, 0, 0))
    rows = jnp.repeat(mod, t_per_batch, axis=0)
    return rows, pl.BlockSpec((tm, d), lambda i, *_: (i, 0))


def _ada_kernel(c_ref, w_ref, b_ref, o_ref):
    c = c_ref[...]
    s = c * jax.nn.sigmoid(c)
    o_ref[...] = jnp.dot(s.astype(BF16), w_ref[...].astype(BF16),
                         preferred_element_type=F32) + b_ref[...]


def _ada(c, w, b):
    bc, d = c.shape
    n = w.shape[1]
    tn = min(1024, n)
    return pl.pallas_call(
        _ada_kernel,
        grid=(n // tn,),
        in_specs=[pl.BlockSpec((bc, d), lambda j: (0, 0)),
                  pl.BlockSpec((d, tn), lambda j: (0, j)),
                  pl.BlockSpec((1, tn), lambda j: (0, j))],
        out_specs=pl.BlockSpec((bc, tn), lambda j: (0, j)),
        out_shape=jax.ShapeDtypeStruct((bc, n), F32),
        compiler_params=_cparams("parallel"),
        name="ada",
    )(c, w, b.reshape(1, n))


def _modmm_kernel(x_ref, sc_ref, sh_ref, w_ref, o_ref, hb_ref):
    @pl.when(pl.program_id(1) == 0)
    def _():
        hb_ref[...] = (x_ref[...] * (1.0 + sc_ref[...]) + sh_ref[...]).astype(BF16)

    o_ref[...] = jnp.dot(hb_ref[...], w_ref[...], preferred_element_type=F32).astype(o_ref.dtype)


def _modmm(x2, sc, sh, w, t_per_batch, out_dtype=F32, name="modmm"):
    n, d = x2.shape
    c = w.shape[1]
    tm = _tile(n, 1024)
    tn = _tile(c, 1024)
    sc_op, sc_spec = _mod_operand(sc, n, t_per_batch, tm)
    sh_op, sh_spec = _mod_operand(sh, n, t_per_batch, tm)
    return pl.pallas_call(
        _modmm_kernel,
        grid=(n // tm, c // tn),
        in_specs=[pl.BlockSpec((tm, d), lambda i, j: (i, 0)), sc_spec, sh_spec,
                  pl.BlockSpec((d, tn), lambda i, j: (0, j))],
        out_specs=pl.BlockSpec((tm, tn), lambda i, j: (i, j)),
        out_shape=jax.ShapeDtypeStruct((n, c), out_dtype),
        scratch_shapes=[pltpu.VMEM((tm, d), BF16)],
        compiler_params=_cparams("parallel", "arbitrary"),
        name=name,
    )(x2, sc_op, sh_op, w)


def _loga_kernel(x_ref, sc_ref, sh_ref, wgf_ref, wf_ref, bf_ref, o_ref):
    h = (x_ref[...] * (1.0 + sc_ref[...]) + sh_ref[...]).astype(BF16)
    gf = jnp.dot(h, wgf_ref[...], preferred_element_type=F32)
    z = jnp.dot(gf.astype(BF16), wf_ref[...], preferred_element_type=F32) + bf_ref[...]
    o_ref[...] = _log_sigmoid(z) / GLA_TAU


def _loga(x2, sc, sh, w_gf, w_f, b_f, t_per_batch):
    n, d = x2.shape
    rp = w_gf.shape[1]
    qk = w_f.shape[1]
    tm = min(512, n)
    sc_op, sc_spec = _mod_operand(sc, n, t_per_batch, tm)
    sh_op, sh_spec = _mod_operand(sh, n, t_per_batch, tm)
    return pl.pallas_call(
        _loga_kernel,
        grid=(n // tm,),
        in_specs=[pl.BlockSpec((tm, d), lambda i: (i, 0)), sc_spec, sh_spec,
                  pl.BlockSpec((d, rp), lambda i: (0, 0)),
                  pl.BlockSpec((rp, qk), lambda i: (0, 0)),
                  pl.BlockSpec((1, qk), lambda i: (0, 0))],
        out_specs=pl.BlockSpec((tm, qk), lambda i: (i, 0)),
        out_shape=jax.ShapeDtypeStruct((n, qk), F32),
        compiler_params=_cparams("parallel"),
        name="loga",
    )(x2, sc_op, sh_op, w_gf, w_f, b_f.reshape(1, qk))


def _gla_kernel(*refs, chunk, n_chunks, has_s0, q_scale, hp):
    if has_s0:
        q_ref, k_ref, v_ref, gg_ref, la_ref, gn_ref, s0_ref, y_ref, sfin_ref, st_ref = refs
    else:
        q_ref, k_ref, v_ref, gg_ref, la_ref, gn_ref, y_ref, sfin_ref, st_ref = refs
    c_idx = pl.program_id(2)
    dk = q_ref.shape[-1] // hp
    dv = v_ref.shape[-1] // hp
    n_sub = chunk // GLA_SUB

    @pl.when(c_idx == 0)
    def _():
        for hh in range(hp):
            if has_s0:
                st_ref[hh] = s0_ref[hh].T
            else:
                st_ref[hh] = jnp.zeros((dv, dk), F32)

    row_c = lax.broadcasted_iota(jnp.int32, (chunk, chunk), 0)
    col_c = lax.broadcasted_iota(jnp.int32, (chunk, chunk), 1)
    tri_incl = (row_c >= col_c).astype(F32)
    col_s = lax.broadcasted_iota(jnp.int32, (GLA_SUB, chunk), 1)
    gnorm = gn_ref[...]

    def head_chunk(r0, hh):
        kcols = slice(hh * dk, (hh + 1) * dk)
        vcols = slice(hh * dv, (hh + 1) * dv)
        q = q_ref[pl.ds(r0, chunk), kcols] * q_scale
        k = k_ref[pl.ds(r0, chunk), kcols]
        v = v_ref[pl.ds(r0, chunk), vcols]
        la = la_ref[pl.ds(r0, chunk), kcols]
        vb = v.astype(BF16)
        cum = jnp.dot(tri_incl, la, preferred_element_type=F32, precision=lax.Precision.HIGHEST)
        st = st_ref[hh]
        o = lax.dot_general((q * jnp.exp(cum)).astype(BF16), st.astype(BF16),
                            (((1,), (1,)), ((), ())), preferred_element_type=F32)
        sd_rows, q_parts, k_parts = [], [], []
        for i in range(n_sub):
            lo = i * GLA_SUB
            qi, ki, ci_ = q[lo:lo + GLA_SUB], k[lo:lo + GLA_SUB], cum[lo:lo + GLA_SUB]
            sd = jnp.zeros((GLA_SUB, chunk), F32)
            for j in range(GLA_SUB):
                e = jnp.exp(jnp.minimum(ci_ - ci_[j:j + 1], 0.0))
                col = jnp.sum(qi * ki[j:j + 1] * e, axis=1, keepdims=True)
                sd = jnp.where(col_s == lo + j, col, sd)
            sd_rows.append(sd)
            if i > 0:
                ref_row = cum[lo - 1:lo]
                qd = qi * jnp.exp(ci_ - ref_row)
                kd = k[:lo] * jnp.exp(ref_row - cum[:lo])
                q_parts.append(jnp.concatenate(
                    [jnp.zeros((lo, dk), F32), qd]
                    + ([jnp.zeros((chunk - lo - GLA_SUB, dk), F32)] if chunk > lo + GLA_SUB else []), axis=0))
                k_parts.append(jnp.concatenate([kd, jnp.zeros((chunk - lo, dk), F32)], axis=0))
        s = jnp.where(row_c >= col_c, jnp.concatenate(sd_rows, axis=0) if n_sub > 1 else sd_rows[0], 0.0)
        if n_sub > 1:
            s = s + lax.dot_general(jnp.concatenate(q_parts, axis=1).astype(BF16),
                                    jnp.concatenate(k_parts, axis=1).astype(BF16),
                                    (((1,), (1,)), ((), ())), preferred_element_type=F32)
        o = o + jnp.dot(s.astype(BF16), vb, preferred_element_type=F32)
        last = cum[chunk - 1:chunk]
        kdec = (k * jnp.exp(last - cum)).astype(BF16)
        st_ref[hh] = st * jnp.exp(last) + lax.dot_general(
            vb, kdec, (((0,), (0,)), ((), ())), preferred_element_type=F32)
        gg = gg_ref[pl.ds(r0, chunk), vcols]
        rms = lax.rsqrt(jnp.mean(o * o, axis=-1, keepdims=True) + RMS_EPS)
        y_ref[pl.ds(r0, chunk), vcols] = (o * rms * gnorm * (gg * jax.nn.sigmoid(gg))).astype(y_ref.dtype)

    def chunk_body(ci, carry):
        r0 = pl.multiple_of(ci * chunk, chunk)
        for hh in range(hp):
            head_chunk(r0, hh)
        return carry

    lax.fori_loop(0, n_chunks, chunk_body, 0)

    @pl.when(c_idx == pl.num_programs(2) - 1)
    def _():
        for hh in range(hp):
            sfin_ref[hh] = st_ref[hh].T


def _gla(g4, la, gnorm, s0, n_heads, dk, dv):
    b, t, _ = g4.shape
    chunk = min(GLA_CHUNK, t)
    tc = min(512, t)
    n_chunks = tc // chunk
    has_s0 = s0 is not None
    hp = 2 if n_heads % 2 == 0 else 1
    wk, wv = hp * dk, hp * dv
    assert (n_heads * dk) % wk == 0 and (2 * n_heads * dk) % wv == 0
    kq = (n_heads * dk) // wk
    kv = (2 * n_heads * dk) // wv
    kg = kv + n_heads // hp
    in_specs = [pl.BlockSpec((None, tc, wk), lambda bi, h, c: (bi, c, h)),
                pl.BlockSpec((None, tc, wk), lambda bi, h, c: (bi, c, kq + h)),
                pl.BlockSpec((None, tc, wv), lambda bi, h, c: (bi, c, kv + h)),
                pl.BlockSpec((None, tc, wv), lambda bi, h, c: (bi, c, kg + h)),
                pl.BlockSpec((None, tc, wk), lambda bi, h, c: (bi, c, h)),
                pl.BlockSpec((1, dv), lambda bi, h, c: (0, 0))]
    args = [g4, g4, g4, g4, la, gnorm.reshape(1, dv)]
    if has_s0:
        in_specs.append(pl.BlockSpec((None, hp, dk, dv), lambda bi, h, c: (bi, h, 0, 0)))
        args.append(s0)
    kern = functools.partial(_gla_kernel, chunk=chunk, n_chunks=n_chunks, has_s0=has_s0,
                             q_scale=dk ** -0.5, hp=hp)
    return pl.pallas_call(
        kern,
        grid=(b, n_heads // hp, t // tc),
        in_specs=in_specs,
        out_specs=[pl.BlockSpec((None, tc, wv), lambda bi, h, c: (bi, c, h)),
                   pl.BlockSpec((None, hp, dk, dv), lambda bi, h, c: (bi, h, 0, 0))],
        out_shape=[jax.ShapeDtypeStruct((b, t, n_heads * dv), BF16),
                   jax.ShapeDtypeStruct((b, n_heads, dk, dv), F32)],
        scratch_shapes=[pltpu.VMEM((hp, dv, dk), F32)],
        compiler_params=_cparams("parallel", "parallel", "arbitrary"),
        name="gla",
    )(*args)


SB_GROUP = 4
SB_DEAD_LOG2 = -151.0
LOG2_E = 1.4426950408889634


def _sb_split_dot(log_1m, u, u_first):
    x = log_1m.astype(BF16)
    if u_first:
        return jnp.dot(u, x, preferred_element_type=F32)
    return jnp.dot(x, u, preferred_element_type=F32)


def _sb_log2_terms(z2):
    neg = jnp.minimum(z2, 0.0)
    log_beta = neg - jnp.log2(1.0 + jnp.exp2((neg + neg) - z2))
    return log_beta, log_beta - z2


def _sb_scores(qb, kb, u, masked):
    tq, tk = qb.shape[0], kb.shape[0]
    z2 = lax.dot_general(qb, kb, (((1,), (1,)), ((), ())), preferred_element_type=F32)
    log_beta, log_1m = _sb_log2_terms(z2)
    if masked:
        vis = (lax.broadcasted_iota(jnp.int32, (tq, tk), 1)
               < lax.broadcasted_iota(jnp.int32, (tq, tk), 0))
        log_1m = jnp.where(vis, log_1m, 0.0)
    later = _sb_split_dot(log_1m, u, False)
    w = jnp.exp2(log_beta + later)
    if masked:
        w = jnp.where(vis, w, 0.0)
    return w.astype(BF16), later[:, :1] + log_1m[:, :1]


def _sb_group(qb, tiles, run, acc, u):
    parts = [_sb_scores(qb, kb, u, masked) for kb, _, masked in tiles]
    for (w, total), (_, vb, _) in zip(parts, tiles):
        acc = acc + jnp.exp2(run) * jnp.dot(w, vb, preferred_element_type=F32)
        run = run + total
    return run, acc


def _sb_kernel(q_ref, kn_ref, vn_ref, u_ref, o_ref, kb_ref, vb_ref, *, tq, scale):
    i = pl.program_id(2)
    t, dh = kn_ref.shape
    ct = tq

    @pl.when(i == 0)
    def _():
        def cast(c, carry):
            r = pl.multiple_of(c * ct, ct)
            kb_ref[pl.ds(r, ct), :] = kn_ref[pl.ds(r, ct), :].astype(BF16)
            vb_ref[pl.ds(r, ct), :] = vn_ref[pl.ds(r, ct), :].astype(BF16)
            return carry

        lax.fori_loop(0, t // ct, cast, 0)

    qb = (q_ref[...] * (scale * LOG2_E)).astype(BF16)
    u = u_ref[...]

    def tile(kt, masked=False):
        r = pl.multiple_of(kt * tq, tq)
        return kb_ref[pl.ds(r, tq), :], vb_ref[pl.ds(r, tq), :], masked

    zero = (jnp.zeros((tq, 1), F32), jnp.zeros((tq, dh), F32))

    def head_pair(_):
        return _sb_group(qb, [tile(i, True), tile(i - 1)], zero[0], zero[1], u) + (i - 2,)

    def head_alone(_):
        return _sb_group(qb, [tile(i, True)], zero[0], zero[1], u) + (i - 1,)

    run, acc, top = lax.cond(i >= 1, head_pair, head_alone, None)

    def alive(r):
        return jnp.max(r) > SB_DEAD_LOG2

    def rest(run, acc, top):
        left = top + 1
        size = 1
        while size <= SB_GROUP:
            full = size == SB_GROUP
            count = lax.div(left, size) if full else lax.rem(lax.div(left, size), 2)

            def group(c, size=size, top=top):
                first = top - c[0] * size
                r, a = _sb_group(qb, [tile(first - j) for j in range(size)], c[1], c[2], u)
                return c[0] + 1, r, a

            def more(c, count=count):
                return jnp.logical_and(c[0] < count, alive(c[1]))

            _, run, acc = lax.while_loop(more, group, (jnp.int32(0), run, acc))
            top = top - count * size
            size *= 2
        return acc

    acc = lax.cond(jnp.logical_and(top >= 0, alive(run)), rest, lambda run, acc, top: acc,
                   run, acc, top)
    o_ref[...] = acc.astype(o_ref.dtype)


def _stick_breaking(q, k_new, v_new, n_heads, dh):
    b, t, _ = q.shape
    tq = min(256, t)
    u = (lax.broadcasted_iota(jnp.int32, (tq, tq), 0)
         > lax.broadcasted_iota(jnp.int32, (tq, tq), 1)).astype(BF16)
    kern = functools.partial(_sb_kernel, tq=tq, scale=dh ** -0.5)
    return pl.pallas_call(
        kern,
        grid=(b, n_heads, t // tq),
        in_specs=[pl.BlockSpec((None, tq, dh), lambda bi, h, i: (bi, i, h)),
                  pl.BlockSpec((None, t, dh), lambda bi, h, i: (bi, 0, h)),
                  pl.BlockSpec((None, t, dh), lambda bi, h, i: (bi, 0, h)),
                  pl.BlockSpec((tq, tq), lambda bi, h, i: (0, 0))],
        out_specs=pl.BlockSpec((None, tq, dh), lambda bi, h, i: (bi, i, h)),
        out_shape=jax.ShapeDtypeStruct((b, t, n_heads * dh), BF16),
        scratch_shapes=[pltpu.VMEM((t, dh), BF16), pltpu.VMEM((t, dh), BF16)],
        compiler_params=_cparams("parallel", "parallel", "arbitrary"),
        name="stick_breaking",
    )(q, k_new, v_new, u)


def _sbd_kernel(q_ref, kn_ref, vn_ref, kc_hbm, vc_hbm, u_ref, qpos_ref, o_ref,
                qbd_ref, run_ref, acc_ref, kbuf_ref, vbuf_ref, sem, *,
                layer, n_ct, n_heads, dh, t, tk, hg, scale):
    bi = pl.program_id(0)
    rows = tk * n_heads

    def fetch(j):
        r = pl.multiple_of(j * rows, rows)
        return (pltpu.make_async_copy(kc_hbm.at[layer, bi, pl.ds(r, rows), :], kbuf_ref, sem.at[0]),
                pltpu.make_async_copy(vc_hbm.at[layer, bi, pl.ds(r, rows), :], vbuf_ref, sem.at[1]))

    for cp in fetch(n_ct - 1):
        cp.start()
    qbd_ref[...] = jnp.zeros_like(qbd_ref)
    for h in range(n_heads):
        qbd_ref[h * t:(h + 1) * t, h * dh:(h + 1) * dh] = (
            q_ref[:, h * dh:(h + 1) * dh] * (scale * LOG2_E)).astype(BF16)
    run_ref[...] = jnp.zeros_like(run_ref)
    acc_ref[...] = jnp.zeros_like(acc_ref)

    def tile(get_k, get_v, masked):
        zt = None
        for g in range(n_heads // hg):
            kcat = [get_k(g * hg + j) for j in range(hg)]
            kcat = (jnp.concatenate(kcat, axis=1) if hg > 1 else kcat[0]).astype(BF16)
            part = lax.dot_general(kcat, qbd_ref[:, g * hg * dh:(g + 1) * hg * dh],
                                   (((1,), (1,)), ((), ())), preferred_element_type=F32)
            zt = part if zt is None else zt + part
        log_beta, log_1m = _sb_log2_terms(zt)
        vis = None
        if masked:
            vis = lax.broadcasted_iota(jnp.int32, zt.shape, 0) < qpos_ref[...]
            log_1m = jnp.where(vis, log_1m, 0.0)
        later = _sb_split_dot(log_1m, u_ref[...], True)
        w = jnp.exp2(log_beta + (later + run_ref[...]))
        if masked:
            w = jnp.where(vis, w, 0.0)
        run_ref[...] += later[0:1, :] + log_1m[0:1, :]
        wt = w.T
        for h in range(n_heads):
            acc_ref[:, h * dh:(h + 1) * dh] += jnp.dot(
                wt[h * t:(h + 1) * t, :].astype(BF16), get_v(h).astype(BF16),
                preferred_element_type=F32)
        return (jnp.max(run_ref[...]) > SB_DEAD_LOG2).astype(jnp.int32)

    pad = jnp.zeros((tk - t, dh), F32)
    alive = tile(lambda h: jnp.concatenate([kn_ref[:, h * dh:(h + 1) * dh], pad], axis=0),
                 lambda h: jnp.concatenate([vn_ref[:, h * dh:(h + 1) * dh], pad], axis=0), True)
    for cp in fetch(n_ct - 1):
        cp.wait()

    def cache_tile(c):
        n = c[0]
        still = tile(lambda h: kbuf_ref[pl.ds(h, tk, stride=n_heads), :],
                     lambda h: vbuf_ref[pl.ds(h, tk, stride=n_heads), :], False)

        @pl.when(jnp.logical_and(still > 0, n + 1 < n_ct))
        def _():
            for cp in fetch(n_ct - 2 - n):
                cp.start()
            for cp in fetch(n_ct - 2 - n):
                cp.wait()

        return n + 1, still

    lax.while_loop(lambda c: jnp.logical_and(c[0] < n_ct, c[1] > 0), cache_tile, (jnp.int32(0), alive))
    o_ref[...] = acc_ref[...].astype(o_ref.dtype)


def _stick_breaking_decode(q, k_new, v_new, cache_k, cache_v, layer, n_heads, dh):
    b, t, hd = q.shape
    tp = cache_k.shape[2]
    tk = min(256, tp)
    n_ct = tp // tk
    nq = n_heads * t
    nqp = -(-nq // LANES) * LANES
    hg = max(1, 2 * LANES // dh)
    if n_heads % hg:
        hg = 1
    u = (lax.broadcasted_iota(jnp.int32, (tk, tk), 1)
         > lax.broadcasted_iota(jnp.int32, (tk, tk), 0)).astype(BF16)
    col = jnp.arange(nqp, dtype=jnp.int32)
    qpos = jnp.where(col < nq, col % t, 0).reshape(1, nqp)
    ck = cache_k.reshape(cache_k.shape[0], b, tp * n_heads, dh)
    cv = cache_v.reshape(cache_v.shape[0], b, tp * n_heads, dh)
    cache_spec = pl.BlockSpec(memory_space=pl.ANY)
    new_spec = pl.BlockSpec((None, t, hd), lambda bi: (bi, 0, 0))
    kern = functools.partial(_sbd_kernel, layer=layer, n_ct=n_ct, n_heads=n_heads, dh=dh, t=t, tk=tk,
                             hg=hg, scale=dh ** -0.5)
    return pl.pallas_call(
        kern,
        grid=(b,),
        in_specs=[new_spec, new_spec, new_spec, cache_spec, cache_spec,
                  pl.BlockSpec((tk, tk), lambda bi: (0, 0)),
                  pl.BlockSpec((1, nqp), lambda bi: (0, 0))],
        out_specs=new_spec,
        out_shape=jax.ShapeDtypeStruct((b, t, hd), BF16),
        scratch_shapes=[pltpu.VMEM((nqp, hd), BF16), pltpu.VMEM((1, nqp), F32),
                        pltpu.VMEM((t, hd), F32),
                        pltpu.VMEM((tk * n_heads, dh), F32), pltpu.VMEM((tk * n_heads, dh), F32),
                        pltpu.SemaphoreType.DMA((2,))],
        compiler_params=_cparams("arbitrary"),
        name="stick_breaking_decode",
    )(q, k_new, v_new, ck, cv, u, qpos)


def _merge_kernel(ya_ref, yb_ref, wa_ref, wb_ref, ga_ref, gb_ref, o_ref):
    a = jnp.dot(ya_ref[...], wa_ref[...], preferred_element_type=F32)
    bb = jnp.dot(yb_ref[...], wb_ref[...], preferred_element_type=F32)
    o_ref[...] = (jax.nn.sigmoid(ga_ref[...]) * a + jax.nn.sigmoid(gb_ref[...]) * bb).astype(o_ref.dtype)


def _merge(ya, yb, wa, wb, gates):
    n, da = ya.shape
    db = yb.shape[1]
    d = wa.shape[1]
    tm = min(1024, n)
    tn = min(512, d)
    nj = d // tn
    return pl.pallas_call(
        _merge_kernel,
        grid=(n // tm, nj),
        in_specs=[pl.BlockSpec((tm, da), lambda i, j: (i, 0)),
                  pl.BlockSpec((tm, db), lambda i, j: (i, 0)),
                  pl.BlockSpec((da, tn), lambda i, j: (0, j)),
                  pl.BlockSpec((db, tn), lambda i, j: (0, j)),
                  pl.BlockSpec((tm, tn), lambda i, j: (i, j)),
                  pl.BlockSpec((tm, tn), lambda i, j: (i, nj + j))],
        out_specs=pl.BlockSpec((tm, tn), lambda i, j: (i, j)),
        out_shape=jax.ShapeDtypeStruct((n, d), BF16),
        compiler_params=_cparams("parallel", "arbitrary"),
        name="merge",
    )(ya, yb, wa, wb, gates, gates)


def _post_kernel(m_ref, wo_ref, x_ref, g1_ref, sc_ref, sh_ref, lg_ref, lb_ref, wr_ref, br_ref,
                 x1_ref, h2_ref, lo_ref, *, alpha):
    mix = jnp.dot(m_ref[...], wo_ref[...], preferred_element_type=F32)
    x1 = _layernorm(alpha * x_ref[...] + g1_ref[...] * mix, lg_ref[...], lb_ref[...])
    h2 = x1 * (1.0 + sc_ref[...]) + sh_ref[...]
    x1_ref[...] = x1
    h2_ref[...] = h2
    lo_ref[...] = jnp.dot(h2.astype(BF16), wr_ref[...], preferred_element_type=F32) + br_ref[...]


def _post(merged, wo, x2, g1, sc2, sh2, ln_g, ln_b, wr, br, t_per_batch, alpha):
    n, d = x2.shape
    e = wr.shape[1]
    tm = min(256, n)
    g1_op, g1_spec = _mod_operand(g1, n, t_per_batch, tm)
    sc_op, sc_spec = _mod_operand(sc2, n, t_per_batch, tm)
    sh_op, sh_spec = _mod_operand(sh2, n, t_per_batch, tm)
    row = lambda i: (i, 0)
    fix = lambda i: (0, 0)
    return pl.pallas_call(
        functools.partial(_post_kernel, alpha=alpha),
        grid=(n // tm,),
        in_specs=[pl.BlockSpec((tm, d), row), pl.BlockSpec((d, d), fix), pl.BlockSpec((tm, d), row),
                  g1_spec, sc_spec, sh_spec,
                  pl.BlockSpec((1, d), fix), pl.BlockSpec((1, d), fix),
                  pl.BlockSpec((d, e), fix), pl.BlockSpec((1, e), fix)],
        out_specs=[pl.BlockSpec((tm, d), row), pl.BlockSpec((tm, d), row), pl.BlockSpec((tm, e), row)],
        out_shape=[jax.ShapeDtypeStruct((n, d), F32), jax.ShapeDtypeStruct((n, d), F32),
                   jax.ShapeDtypeStruct((n, e), F32)],
        compiler_params=_cparams("parallel"),
        name="post",
    )(merged, wo, x2, g1_op, sc_op, sh_op, ln_g.reshape(1, d), ln_b.reshape(1, d), wr,
      br.reshape(1, e))


def _cols_to_lanes(cols):
    tm = cols[0].shape[0]
    lane = lax.broadcasted_iota(jnp.int32, (tm, len(cols)), 1)
    out = jnp.broadcast_to(cols[0], (tm, len(cols)))
    for kk in range(1, len(cols)):
        out = jnp.where(lane == kk, cols[kk], out)
    return out


def _route_kernel(lo_ref, idx_ref, w_ref, rank_ref, cnt_ref, carry_ref):
    @pl.when(pl.program_id(0) == 0)
    def _():
        carry_ref[...] = jnp.zeros_like(carry_ref)

    lg = lo_ref[...]
    tm, e = lg.shape
    lane = lax.broadcasted_iota(jnp.int32, (tm, e), 1).astype(F32)
    vals, idxs = [], []
    chosen = jnp.zeros((tm, e), F32)
    for _ in range(TOP_K):
        m = jnp.max(lg, axis=-1, keepdims=True)
        ik = jnp.min(jnp.where(lg == m, lane, float(e)), axis=-1, keepdims=True)
        sel = lane == ik
        vals.append(m)
        idxs.append(ik)
        chosen = jnp.where(sel, 1.0, chosen)
        lg = jnp.where(sel, -jnp.inf, lg)
    ex = [jnp.exp(v - vals[0]) for v in vals]
    tot = ex[0]
    for t_ in ex[1:]:
        tot = tot + t_
    before = (lax.broadcasted_iota(jnp.int32, (tm, tm), 1)
              < lax.broadcasted_iota(jnp.int32, (tm, tm), 0)).astype(BF16)
    pos = jnp.dot(before, chosen.astype(BF16), preferred_element_type=F32) + carry_ref[...]
    ranks = [jnp.sum(jnp.where(lane == ik, pos, 0.0), axis=-1, keepdims=True) for ik in idxs]
    idx_ref[...] = _cols_to_lanes(idxs).astype(jnp.int32)
    w_ref[...] = _cols_to_lanes([x / tot for x in ex])
    rank_ref[...] = _cols_to_lanes(ranks).astype(jnp.int32)
    carry_ref[...] = carry_ref[...] + jnp.sum(chosen, axis=0, keepdims=True)
    cnt_ref[...] = carry_ref[...].astype(jnp.int32)


def _route(logits):
    n, e = logits.shape
    tm = min(256, n)
    row = lambda i: (i, 0)
    return pl.pallas_call(
        _route_kernel,
        grid=(n // tm,),
        in_specs=[pl.BlockSpec((tm, e), row)],
        out_specs=[pl.BlockSpec((tm, TOP_K), row), pl.BlockSpec((tm, TOP_K), row),
                   pl.BlockSpec((tm, TOP_K), row), pl.BlockSpec((1, e), lambda i: (0, 0))],
        out_shape=[jax.ShapeDtypeStruct((n, TOP_K), jnp.int32), jax.ShapeDtypeStruct((n, TOP_K), F32),
                   jax.ShapeDtypeStruct((n, TOP_K), jnp.int32), jax.ShapeDtypeStruct((1, e), jnp.int32)],
        scratch_shapes=[pltpu.VMEM((1, e), F32)],
        compiler_params=_cparams("arbitrary"),
        name="route",
    )(logits)


def _dest_kernel(idx_ref, rank_ref, start_ref, o_ref):
    idx = idx_ref[...]
    tm = idx.shape[0]
    e = start_ref.shape[1]
    lane = lax.broadcasted_iota(jnp.int32, (tm, e), 1)
    start = start_ref[...].astype(F32)
    cols = [jnp.sum(jnp.where(lane == idx[:, kk:kk + 1], start, 0.0), axis=-1, keepdims=True)
            for kk in range(TOP_K)]
    o_ref[...] = _cols_to_lanes(cols).astype(jnp.int32) + rank_ref[...]


def _dest(idx, rank, group_start):
    n = idx.shape[0]
    e = group_start.shape[1]
    tm = min(256, n)
    row = lambda i: (i, 0)
    return pl.pallas_call(
        _dest_kernel,
        grid=(n // tm,),
        in_specs=[pl.BlockSpec((tm, TOP_K), row), pl.BlockSpec((tm, TOP_K), row),
                  pl.BlockSpec((1, e), lambda i: (0, 0))],
        out_specs=pl.BlockSpec((tm, TOP_K), row),
        out_shape=jax.ShapeDtypeStruct((n, TOP_K), jnp.int32),
        compiler_params=_cparams("parallel"),
        name="dest",
    )(idx, rank, group_start)


def _split_gate_up_kernel(w_ref, p_ref, g_ref, u_ref):
    x = w_ref[...].astype(BF16)
    p = p_ref[...]
    width = p.shape[0]
    half = width // 2
    for g in range(x.shape[1] // width):
        res = jnp.dot(x[:, g * width:(g + 1) * width], p, preferred_element_type=F32)
        g_ref[:, g * half:(g + 1) * half] = res[:, :half].astype(BF16)
        u_ref[:, g * half:(g + 1) * half] = res[:, half:].astype(BF16)


def _split_gate_up(w):
    e, d, f2 = w.shape
    width = 2 * LANES
    tr = min(1024, d)
    tc = min(1024, f2)
    r = lax.broadcasted_iota(jnp.int32, (width, width), 0)
    c = lax.broadcasted_iota(jnp.int32, (width, width), 1)
    perm = jnp.where(c < LANES, r == 2 * c, r == 2 * (c - LANES) + 1).astype(BF16)
    return pl.pallas_call(
        _split_gate_up_kernel,
        grid=(e, d // tr, f2 // tc),
        in_specs=[pl.BlockSpec((None, tr, tc), lambda ei, ri, ci: (ei, ri, ci)),
                  pl.BlockSpec((width, width), lambda ei, ri, ci: (0, 0))],
        out_specs=[pl.BlockSpec((None, tr, tc // 2), lambda ei, ri, ci: (ei, ri, ci)),
                   pl.BlockSpec((None, tr, tc // 2), lambda ei, ri, ci: (ei, ri, ci))],
        out_shape=[jax.ShapeDtypeStruct((e, d, f2 // 2), BF16),
                   jax.ShapeDtypeStruct((e, d, f2 // 2), BF16)],
        compiler_params=_cparams("parallel", "parallel", "parallel"),
        name="split_gate_up",
    )(w, perm)


def _row_copy(src_ref, src_row, dst_ref, dst_row, sem):
    return pltpu.make_async_copy(src_ref.at[pl.ds(src_row, 1), :], dst_ref.at[pl.ds(dst_row, 1), :], sem)


def _dispatch_kernel(dest_ref, h_ref, xs_in_ref, xs_ref, sem):
    del xs_in_ref
    tm = h_ref.shape[0]

    def issue(g, c):
        base = pl.multiple_of(g * SUBLANES, SUBLANES)
        for r in range(SUBLANES):
            for kk in range(TOP_K):
                _row_copy(h_ref, base + r, xs_ref, dest_ref[(base + r) * TOP_K + kk], sem).start()
        return c

    lax.fori_loop(0, tm // SUBLANES, issue, 0)

    def drain(t, c):
        for kk in range(TOP_K):
            _row_copy(h_ref, 0, xs_ref, 0, sem).wait()
        return c

    lax.fori_loop(0, tm, drain, 0)


def _dispatch(dest_flat, h2, n_rows):
    n, d = h2.shape
    tm = min(256, n)
    xs0 = jnp.zeros((n_rows, d), F32)
    return pl.pallas_call(
        _dispatch_kernel,
        grid=(n // tm,),
        in_specs=[pl.BlockSpec((tm * TOP_K,), lambda i: (i,), memory_space=pltpu.SMEM),
                  pl.BlockSpec((tm, d), lambda i: (i, 0)),
                  pl.BlockSpec(memory_space=pl.ANY)],
        out_specs=pl.BlockSpec(memory_space=pl.ANY),
        out_shape=jax.ShapeDtypeStruct((n_rows, d), F32),
        scratch_shapes=[pltpu.SemaphoreType.DMA(())],
        input_output_aliases={2: 0},
        compiler_params=_cparams("arbitrary"),
        name="dispatch",
    )(dest_flat, h2, xs0)


EXPERT_ROW_SPLIT = 1


def _experts_kernel(be_ref, nu_ref, xs_ref, wg_ref, wu_ref, bg_ref, bu_ref, wd_ref, bd_ref,
                    y_ref, xb_ref):
    del be_ref
    j = pl.program_id(0)
    f = pl.program_id(1)

    @pl.when(j < nu_ref[0])
    def _():
        @pl.when(f == 0)
        def _():
            xb_ref[...] = xs_ref[...].astype(BF16)
            y_ref[...] = jnp.broadcast_to(bd_ref[...], y_ref.shape)

        rows = xb_ref.shape[0] // EXPERT_ROW_SPLIT
        for r in range(EXPERT_ROW_SPLIT):
            sl = slice(r * rows, (r + 1) * rows)
            xb = xb_ref[sl, :]
            gate = jnp.dot(xb, wg_ref[...], preferred_element_type=F32) + bg_ref[...]
            up = jnp.dot(xb, wu_ref[...], preferred_element_type=F32) + bu_ref[...]
            gate = jnp.minimum(gate, SWIGLU_LIMIT)
            up = jnp.clip(up, -SWIGLU_LIMIT, SWIGLU_LIMIT)
            act = (up + 1.0) * (gate * jax.nn.sigmoid(SWIGLU_ALPHA * gate))
            y_ref[sl, :] += jnp.dot(act.astype(BF16), wd_ref[...], preferred_element_type=F32)

    @pl.when(jnp.logical_and(j >= nu_ref[0], f == 0))
    def _():
        y_ref[...] = jnp.zeros_like(y_ref)


def _experts(xs, block_e, n_used, wg, wu, bg, bu, wd, bd, rb):
    n_rows, d = xs.shape
    e, _, dff = wg.shape
    nb = n_rows // rb
    tf = _tile(dff, 1024)
    nf = dff // tf

    def live(j, nu):
        return jnp.minimum(j, nu[0] - 1)

    def fidx(j, f, nu):
        return jnp.where(j < nu[0], f, nf - 1)

    grid_spec = pltpu.PrefetchScalarGridSpec(
        num_scalar_prefetch=2,
        grid=(nb, nf),
        in_specs=[
            pl.BlockSpec((rb, d), lambda j, f, be, nu: (live(j, nu), 0)),
            pl.BlockSpec((None, d, tf), lambda j, f, be, nu: (be[live(j, nu)], 0, fidx(j, f, nu))),
            pl.BlockSpec((None, d, tf), lambda j, f, be, nu: (be[live(j, nu)], 0, fidx(j, f, nu))),
            pl.BlockSpec((None, 1, tf), lambda j, f, be, nu: (be[live(j, nu)], 0, fidx(j, f, nu))),
            pl.BlockSpec((None, 1, tf), lambda j, f, be, nu: (be[live(j, nu)], 0, fidx(j, f, nu))),
            pl.BlockSpec((None, tf, d), lambda j, f, be, nu: (be[live(j, nu)], fidx(j, f, nu), 0)),
            pl.BlockSpec((None, 1, d), lambda j, f, be, nu: (be[live(j, nu)], 0, 0)),
        ],
        out_specs=pl.BlockSpec((rb, d), lambda j, f, be, nu: (j, 0)),
        scratch_shapes=[pltpu.VMEM((rb, d), BF16)],
    )
    return pl.pallas_call(
        _experts_kernel,
        grid_spec=grid_spec,
        out_shape=jax.ShapeDtypeStruct((n_rows, d), F32),
        compiler_params=_cparams("arbitrary", "arbitrary"),
        name="experts",
    )(block_e, n_used, xs, wg, wu, bg, bu, wd, bd)


def _combine_kernel(dest_ref, y_ref, w_ref, x1_ref, g2_ref, lg_ref, lb_ref, o_ref, buf_ref, sem,
                    *, alpha):
    tm = x1_ref.shape[0]

    def issue(g, c):
        base = pl.multiple_of(g * SUBLANES, SUBLANES)
        for r in range(SUBLANES):
            for kk in range(TOP_K):
                _row_copy(y_ref, dest_ref[(base + r) * TOP_K + kk], buf_ref.at[kk], base + r, sem).start()
        return c

    lax.fori_loop(0, tm // SUBLANES, issue, 0)

    def drain(t, c):
        for kk in range(TOP_K):
            _row_copy(y_ref, 0, buf_ref.at[kk], 0, sem).wait()
        return c

    lax.fori_loop(0, tm, drain, 0)
    w = w_ref[...]
    ffn = w[:, 0:1] * buf_ref[0]
    for kk in range(1, TOP_K):
        ffn = ffn + w[:, kk:kk + 1] * buf_ref[kk]
    o_ref[...] = _layernorm(alpha * x1_ref[...] + g2_ref[...] * ffn, lg_ref[...], lb_ref[...])


def _combine(dest_flat, y, top_w, x1, g2, ln_g, ln_b, t_per_batch, alpha):
    n, d = x1.shape
    tm = min(256, n)
    g2_op, g2_spec = _mod_operand(g2, n, t_per_batch, tm)
    row = lambda i: (i, 0)
    fix = lambda i: (0, 0)
    return pl.pallas_call(
        functools.partial(_combine_kernel, alpha=alpha),
        grid=(n // tm,),
        in_specs=[pl.BlockSpec((tm * TOP_K,), lambda i: (i,), memory_space=pltpu.SMEM),
                  pl.BlockSpec(memory_space=pl.ANY),
                  pl.BlockSpec((tm, TOP_K), row), pl.BlockSpec((tm, d), row), g2_spec,
                  pl.BlockSpec((1, d), fix), pl.BlockSpec((1, d), fix)],
        out_specs=pl.BlockSpec((tm, d), row),
        out_shape=jax.ShapeDtypeStruct((n, d), F32),
        scratch_shapes=[pltpu.VMEM((TOP_K, tm, d), F32), pltpu.SemaphoreType.DMA(())],
        compiler_params=_cparams("arbitrary"),
        name="combine",
    )(dest_flat, y, top_w, x1, g2_op, ln_g.reshape(1, d), ln_b.reshape(1, d))


def _moe(h2, logits, x1, g2, ln_g, ln_b, ew, t_per_batch, alpha, rb):
    n, d = h2.shape
    e = logits.shape[1]
    idx, top_w, rank, counts = _route(logits)
    padded = (counts + rb - 1) // rb * rb
    group_end = jnp.cumsum(padded, axis=1)
    dest = _dest(idx, rank, group_end - padded)
    dest_flat = dest.reshape(n * TOP_K)
    nb = -(-(n * TOP_K) // rb) + e
    block_e = jnp.minimum(jnp.searchsorted(group_end[0], jnp.arange(nb, dtype=jnp.int32) * rb,
                                           side='right'), e - 1).astype(jnp.int32)
    n_used = (group_end[0, -1:] // rb).astype(jnp.int32)
    xs = _dispatch(dest_flat, h2, nb * rb)
    y = _experts(xs, block_e, n_used, *ew, rb=rb)
    return _combine(dest_flat, y, top_w, x1, g2, ln_g, ln_b, t_per_batch, alpha)


def _layer(x, mod, s0, cache, p, alpha):
    b, t, d = x.shape
    n = b * t
    sh1, sc1, g1, sh2, sc2, g2 = [mod[:, i * d:(i + 1) * d] for i in range(6)]
    x2 = x.reshape(n, d)
    hg, dk, dv, hs, dh = p['dims']
    g4 = _modmm(x2, sc1, sh1, p['w_g4'], t, name="proj_gla")
    la = _loga(x2, sc1, sh1, p['w_gf'], p['w_gla_f'], p['b_gla_f'], t)
    sq = _modmm(x2, sc1, sh1, p['w_sq'], t, name="proj_sq")
    sk = _modmm(x2, sc1, sh1, p['w_sk'], t, name="proj_sk")
    sv = _modmm(x2, sc1, sh1, p['w_sv'], t, name="proj_sv")
    gates = _modmm(x2, sc1, sh1, p['w_gates'], t, name="proj_gates")
    y_gla, s_fin = _gla(g4.reshape(b, t, -1), la.reshape(b, t, -1), p['gla_norm_g'], s0, hg, dk, dv)
    q3, k3, v3 = sq.reshape(b, t, -1), sk.reshape(b, t, -1), sv.reshape(b, t, -1)
    if cache is None:
        o_sb = _stick_breaking(q3, k3, v3, hs, dh)
    else:
        o_sb = _stick_breaking_decode(q3, k3, v3, cache[0], cache[1], cache[2], hs, dh)
    merged = _merge(y_gla.reshape(n, -1), o_sb.reshape(n, -1), p['w_gla_out'], p['w_sb_out'], gates)
    x1, h2, logits = _post(merged, p['w_o'], x2, g1, sc2, sh2, p['ln1_g'], p['ln1_b'],
                           p['w_router'], p['b_router'], t, alpha)
    rb = 512 if n * TOP_K >= 512 * p['n_experts'] * 4 else 128
    rb = min(rb, n)
    x_out = _moe(h2, logits, x1, g2, p['ln2_g'], p['ln2_b'], p['experts'], t, alpha, rb)
    return (x_out.reshape(b, t, d), s_fin, sk.reshape(b, t, hs, dh), sv.reshape(b, t, hs, dh))


def kernel(x_prompt, x_sample, c_prompt, c_sample, state_gla, cache_sb_k, cache_sb_v, w_ada, b_ada, w_in, w_gla_f, b_gla_f, gla_norm_g, w_gla_out, w_sb_out, w_o, ln1_g, ln1_b, w_router, b_router, w_gate_up, b_gate_up, w_down, b_down, ln2_g, ln2_b):
    depth = w_ada.shape[0]
    d = x_prompt.shape[-1]
    hg, dk, dv = state_gla.shape[2:]
    hs, dh = cache_sb_k.shape[3:]
    rank = w_gla_f.shape[1]
    n_experts = w_router.shape[-1]
    alpha = (2.0 * depth) ** 0.25
    qk, gv, sbw = hg * dk, hg * dv, hs * dh
    o_gf = 2 * qk + 2 * gv
    o_sq = o_gf + rank
    o_sk, o_sv, o_ga = o_sq + sbw, o_sq + 2 * sbw, o_sq + 3 * sbw
    rpad = -(-rank // LANES) * LANES
    n_p = x_prompt.shape[0]
    y_p, y_s = x_prompt, x_sample
    outs = [[] for _ in range(6)]
    for l in range(depth):
        wi = w_in[l]
        p = {
            'dims': (hg, dk, dv, hs, dh), 'n_experts': n_experts,
            'w_g4': wi[:, :o_gf].astype(BF16),
            'w_gf': jnp.pad(wi[:, o_gf:o_sq], ((0, 0), (0, rpad - rank))).astype(BF16),
            'w_gla_f': jnp.pad(w_gla_f[l], ((0, rpad - rank), (0, 0))).astype(BF16),
            'b_gla_f': b_gla_f[l],
            'w_sq': wi[:, o_sq:o_sk].astype(BF16), 'w_sk': wi[:, o_sk:o_sv].astype(BF16),
            'w_sv': wi[:, o_sv:o_ga].astype(BF16), 'w_gates': wi[:, o_ga:].astype(BF16),
            'gla_norm_g': gla_norm_g[l],
            'w_gla_out': w_gla_out[l].astype(BF16), 'w_sb_out': w_sb_out[l].astype(BF16),
            'w_o': w_o[l].astype(BF16), 'ln1_g': ln1_g[l], 'ln1_b': ln1_b[l],
            'w_router': w_router[l].astype(BF16), 'b_router': b_router[l],
            'ln2_g': ln2_g[l], 'ln2_b': ln2_b[l],
            'experts': tuple(_split_gate_up(w_gate_up[l])) + (
                b_gate_up[l][:, None, 0::2], b_gate_up[l][:, None, 1::2],
                w_down[l].astype(BF16), b_down[l][:, None, :]),
        }
        mod = _ada(jnp.concatenate([c_prompt, c_sample], axis=0), w_ada[l], b_ada[l])
        y_s, sf, kn, vn = _layer(y_s, mod[n_p:], state_gla[l], (cache_sb_k, cache_sb_v, l), p, alpha)
        outs[3].append(sf), outs[4].append(kn), outs[5].append(vn)
        y_p, sf, kn, vn = _layer(y_p, mod[:n_p], None, None, p, alpha)
        outs[0].append(sf), outs[1].append(kn), outs[2].append(vn)
    return (y_p, y_s) + tuple(jnp.stack(o) for o in outs)
```

```python
import functools

import jax
import jax.numpy as jnp
from jax import lax
from jax.experimental import pallas as pl
from jax.experimental.pallas import tpu as pltpu

F32 = jnp.float32
BF16 = jnp.bfloat16

GLA_CHUNK = 64
GLA_SUB = 8
GLA_TAU = 16.0
TOP_K = 4
SWIGLU_LIMIT = 7.0
SWIGLU_ALPHA = 1.702
LN_EPS = 1e-5
RMS_EPS = 1e-6
LANES = 128
SUBLANES = 8
VMEM_LIMIT_BYTES = 56 * 1024 * 1024


def _cparams(*sem):
    return pltpu.CompilerParams(dimension_semantics=sem, vmem_limit_bytes=VMEM_LIMIT_BYTES)


def _tile(n, want):
    t = min(want, n)
    while n % t:
        t //= 2
    return t


def _log_sigmoid(x):
    return jnp.minimum(x, 0.0) - jnp.log1p(jnp.exp(-jnp.abs(x)))


def _layernorm(u, g, b):
    mu = jnp.mean(u, axis=-1, keepdims=True)
    d = u - mu
    var = jnp.mean(d * d, axis=-1, keepdims=True)
    return d * lax.rsqrt(var + LN_EPS) * g + b


def _mod_operand(mod, n_tok, t_per_batch, tm):
    b, d = mod.shape
    if t_per_batch % tm == 0:
        per = t_per_batch // tm
        return mod.reshape(b, 1, d), pl.BlockSpec((None, 1, d), lambda i, *_: (i // per, 0, 0))
    rows = jnp.repeat(mod, t_per_batch, axis=0)
    return rows, pl.BlockSpec((tm, d), lambda i, *_: (i, 0))


def _ada_kernel(c_ref, w_ref, b_ref, o_ref):
    c = c_ref[...]
    s = c * jax.nn.sigmoid(c)
    o_ref[...] = jnp.dot(s.astype(BF16), w_ref[...].astype(BF16),
                         preferred_element_type=F32) + b_ref[...]


def _ada(c, w, b):
    bc, d = c.shape
    n = w.shape[1]
    tn = min(1024, n)
    return pl.pallas_call(
        _ada_kernel,
        grid=(n // tn,),
        in_specs=[pl.BlockSpec((bc, d), lambda j: (0, 0)),
                  pl.BlockSpec((d, tn), lambda j: (0, j)),
                  pl.BlockSpec((1, tn), lambda j: (0, j))],
        out_specs=pl.BlockSpec((bc, tn), lambda j: (0, j)),
        out_shape=jax.ShapeDtypeStruct((bc, n), F32),
        compiler_params=_cparams("parallel"),
        name="ada",
    )(c, w, b.reshape(1, n))


def _modmm_kernel(x_ref, sc_ref, sh_ref, w_ref, o_ref, hb_ref):
    @pl.when(pl.program_id(1) == 0)
    def _():
        hb_ref[...] = (x_ref[...] * (1.0 + sc_ref[...]) + sh_ref[...]).astype(BF16)

    o_ref[...] = jnp.dot(hb_ref[...], w_ref[...], preferred_element_type=F32).astype(o_ref.dtype)


def _modmm(x2, sc, sh, w, t_per_batch, out_dtype=F32, name="modmm"):
    n, d = x2.shape
    c = w.shape[1]
    tm = _tile(n, 1024)
    tn = _tile(c, 1024)
    sc_op, sc_spec = _mod_operand(sc, n, t_per_batch, tm)
    sh_op, sh_spec = _mod_operand(sh, n, t_per_batch, tm)
    return pl.pallas_call(
        _modmm_kernel,
        grid=(n // tm, c // tn),
        in_specs=[pl.BlockSpec((tm, d), lambda i, j: (i, 0)), sc_spec, sh_spec,
                  pl.BlockSpec((d, tn), lambda i, j: (0, j))],
        out_specs=pl.BlockSpec((tm, tn), lambda i, j: (i, j)),
        out_shape=jax.ShapeDtypeStruct((n, c), out_dtype),
        scratch_shapes=[pltpu.VMEM((tm, d), BF16)],
        compiler_params=_cparams("parallel", "arbitrary"),
        name=name,
    )(x2, sc_op, sh_op, w)


def _loga_kernel(x_ref, sc_ref, sh_ref, wgf_ref, wf_ref, bf_ref, o_ref):
    h = (x_ref[...] * (1.0 + sc_ref[...]) + sh_ref[...]).astype(BF16)
    gf = jnp.dot(h, wgf_ref[...], preferred_element_type=F32)
    z = jnp.dot(gf.astype(BF16), wf_ref[...], preferred_element_type=F32) + bf_ref[...]
    o_ref[...] = _log_sigmoid(z) / GLA_TAU


def _loga(x2, sc, sh, w_gf, w_f, b_f, t_per_batch):
    n, d = x2.shape
    rp = w_gf.shape[1]
    qk = w_f.shape[1]
    tm = min(512, n)
    sc_op, sc_spec = _mod_operand(sc, n, t_per_batch, tm)
    sh_op, sh_spec = _mod_operand(sh, n, t_per_batch, tm)
    return pl.pallas_call(
        _loga_kernel,
        grid=(n // tm,),
        in_specs=[pl.BlockSpec((tm, d), lambda i: (i, 0)), sc_spec, sh_spec,
                  pl.BlockSpec((d, rp), lambda i: (0, 0)),
                  pl.BlockSpec((rp, qk), lambda i: (0, 0)),
                  pl.BlockSpec((1, qk), lambda i: (0, 0))],
        out_specs=pl.BlockSpec((tm, qk), lambda i: (i, 0)),
        out_shape=jax.ShapeDtypeStruct((n, qk), F32),
        compiler_params=_cparams("parallel"),
        name="loga",
    )(x2, sc_op, sh_op, w_gf, w_f, b_f.reshape(1, qk))


def _gla_kernel(*refs, chunk, n_chunks, has_s0, q_scale, hp):
    if has_s0:
        q_ref, k_ref, v_ref, gg_ref, la_ref, gn_ref, s0_ref, y_ref, sfin_ref, st_ref = refs
    else:
        q_ref, k_ref, v_ref, gg_ref, la_ref, gn_ref, y_ref, sfin_ref, st_ref = refs
    c_idx = pl.program_id(2)
    dk = q_ref.shape[-1] // hp
    dv = v_ref.shape[-1] // hp
    n_sub = chunk // GLA_SUB

    @pl.when(c_idx == 0)
    def _():
        for hh in range(hp):
            if has_s0:
                st_ref[hh] = s0_ref[hh].T
            else:
                st_ref[hh] = jnp.zeros((dv, dk), F32)

    row_c = lax.broadcasted_iota(jnp.int32, (chunk, chunk), 0)
    col_c = lax.broadcasted_iota(jnp.int32, (chunk, chunk), 1)
    tri_incl = (row_c >= col_c).astype(F32)
    col_s = lax.broadcasted_iota(jnp.int32, (GLA_SUB, chunk), 1)
    gnorm = gn_ref[...]

    def head_chunk(r0, hh):
        kcols = slice(hh * dk, (hh + 1) * dk)
        vcols = slice(hh * dv, (hh + 1) * dv)
        q = q_ref[pl.ds(r0, chunk), kcols] * q_scale
        k = k_ref[pl.ds(r0, chunk), kcols]
        v = v_ref[pl.ds(r0, chunk), vcols]
        la = la_ref[pl.ds(r0, chunk), kcols]
        vb = v.astype(BF16)
        cum = jnp.dot(tri_incl, la, preferred_element_type=F32, precision=lax.Precision.HIGHEST)
        st = st_ref[hh]
        o = lax.dot_general((q * jnp.exp(cum)).astype(BF16), st.astype(BF16),
                            (((1,), (1,)), ((), ())), preferred_element_type=F32)
        sd_rows, q_parts, k_parts = [], [], []
        for i in range(n_sub):
            lo = i * GLA_SUB
            qi, ki, ci_ = q[lo:lo + GLA_SUB], k[lo:lo + GLA_SUB], cum[lo:lo + GLA_SUB]
            sd = jnp.zeros((GLA_SUB, chunk), F32)
            for j in range(GLA_SUB):
                e = jnp.exp(jnp.minimum(ci_ - ci_[j:j + 1], 0.0))
                col = jnp.sum(qi * ki[j:j + 1] * e, axis=1, keepdims=True)
                sd = jnp.where(col_s == lo + j, col, sd)
            sd_rows.append(sd)
            if i > 0:
                ref_row = cum[lo - 1:lo]
                qd = qi * jnp.exp(ci_ - ref_row)
                kd = k[:lo] * jnp.exp(ref_row - cum[:lo])
                q_parts.append(jnp.concatenate(
                    [jnp.zeros((lo, dk), F32), qd]
                    + ([jnp.zeros((chunk - lo - GLA_SUB, dk), F32)] if chunk > lo + GLA_SUB else []), axis=0))
                k_parts.append(jnp.concatenate([kd, jnp.zeros((chunk - lo, dk), F32)], axis=0))
        s = jnp.where(row_c >= col_c, jnp.concatenate(sd_rows, axis=0) if n_sub > 1 else sd_rows[0], 0.0)
        if n_sub > 1:
            s = s + lax.dot_general(jnp.concatenate(q_parts, axis=1).astype(BF16),
                                    jnp.concatenate(k_parts, axis=1).astype(BF16),
                                    (((1,), (1,)), ((), ())), preferred_element_type=F32)
        o = o + jnp.dot(s.astype(BF16), vb, preferred_element_type=F32)
        last = cum[chunk - 1:chunk]
        kdec = (k * jnp.exp(last - cum)).astype(BF16)
        st_ref[hh] = st * jnp.exp(last) + lax.dot_general(
            vb, kdec, (((0,), (0,)), ((), ())), preferred_element_type=F32)
        gg = gg_ref[pl.ds(r0, chunk), vcols]
        rms = lax.rsqrt(jnp.mean(o * o, axis=-1, keepdims=True) + RMS_EPS)
        y_ref[pl.ds(r0, chunk), vcols] = (o * rms * gnorm * (gg * jax.nn.sigmoid(gg))).astype(y_ref.dtype)

    def chunk_body(ci, carry):
        r0 = pl.multiple_of(ci * chunk, chunk)
        for hh in range(hp):
            head_chunk(r0, hh)
        return carry

    lax.fori_loop(0, n_chunks, chunk_body, 0)

    @pl.when(c_idx == pl.num_programs(2) - 1)
    def _():
        for hh in range(hp):
            sfin_ref[hh] = st_ref[hh].T


def _gla(g4, la, gnorm, s0, n_heads, dk, dv):
    b, t, _ = g4.shape
    chunk = min(GLA_CHUNK, t)
    tc = min(512, t)
    n_chunks = tc // chunk
    has_s0 = s0 is not None
    hp = 2 if n_heads % 2 == 0 else 1
    wk, wv = hp * dk, hp * dv
    assert (n_heads * dk) % wk == 0 and (2 * n_heads * dk) % wv == 0
    kq = (n_heads * dk) // wk
    kv = (2 * n_heads * dk) // wv
    kg = kv + n_heads // hp
    in_specs = [pl.BlockSpec((None, tc, wk), lambda bi, h, c: (bi, c, h)),
                pl.BlockSpec((None, tc, wk), lambda bi, h, c: (bi, c, kq + h)),
                pl.BlockSpec((None, tc, wv), lambda bi, h, c: (bi, c, kv + h)),
                pl.BlockSpec((None, tc, wv), lambda bi, h, c: (bi, c, kg + h)),
                pl.BlockSpec((None, tc, wk), lambda bi, h, c: (bi, c, h)),
                pl.BlockSpec((1, dv), lambda bi, h, c: (0, 0))]
    args = [g4, g4, g4, g4, la, gnorm.reshape(1, dv)]
    if has_s0:
        in_specs.append(pl.BlockSpec((None, hp, dk, dv), lambda bi, h, c: (bi, h, 0, 0)))
        args.append(s0)
    kern = functools.partial(_gla_kernel, chunk=chunk, n_chunks=n_chunks, has_s0=has_s0,
                             q_scale=dk ** -0.5, hp=hp)
    return pl.pallas_call(
        kern,
        grid=(b, n_heads // hp, t // tc),
        in_specs=in_specs,
        out_specs=[pl.BlockSpec((None, tc, wv), lambda bi, h, c: (bi, c, h)),
                   pl.BlockSpec((None, hp, dk, dv), lambda bi, h, c: (bi, h, 0, 0))],
        out_shape=[jax.ShapeDtypeStruct((b, t, n_heads * dv), BF16),
                   jax.ShapeDtypeStruct((b, n_heads, dk, dv), F32)],
        scratch_shapes=[pltpu.VMEM((hp, dv, dk), F32)],
        compiler_params=_cparams("parallel", "parallel", "arbitrary"),
        name="gla",
    )(*args)


SB_GROUP = 4
SB_DEAD_LOG2 = -151.0
LOG2_E = 1.4426950408889634


def _sb_split_dot(log_1m, u, u_first):
    x = log_1m.astype(BF16)
    if u_first:
        return jnp.dot(u, x, preferred_element_type=F32)
    return jnp.dot(x, u, preferred_element_type=F32)


def _sb_log2_terms(z2):
    neg = jnp.minimum(z2, 0.0)
    log_beta = neg - jnp.log2(1.0 + jnp.exp2((neg + neg) - z2))
    return log_beta, log_beta - z2


def _sb_scores(qb, kb, u, masked):
    tq, tk = qb.shape[0], kb.shape[0]
    z2 = lax.dot_general(qb, kb, (((1,), (1,)), ((), ())), preferred_element_type=F32)
    log_beta, log_1m = _sb_log2_terms(z2)
    if masked:
        vis = (lax.broadcasted_iota(jnp.int32, (tq, tk), 1)
               < lax.broadcasted_iota(jnp.int32, (tq, tk), 0))
        log_1m = jnp.where(vis, log_1m, 0.0)
    later = _sb_split_dot(log_1m, u, False)
    w = jnp.exp2(log_beta + later)
    if masked:
        w = jnp.where(vis, w, 0.0)
    return w.astype(BF16), later[:, :1] + log_1m[:, :1]


def _sb_group(qb, tiles, run, acc, u):
    parts = [_sb_scores(qb, kb, u, masked) for kb, _, masked in tiles]
    for (w, total), (_, vb, _) in zip(parts, tiles):
        acc = acc + jnp.exp2(run) * jnp.dot(w, vb, preferred_element_type=F32)
        run = run + total
    return run, acc


def _sb_kernel(q_ref, kn_ref, vn_ref, u_ref, o_ref, kb_ref, vb_ref, *, tq, scale):
    i = pl.program_id(2)
    t, dh = kn_ref.shape
    ct = tq

    @pl.when(i == 0)
    def _():
        def cast(c, carry):
            r = pl.multiple_of(c * ct, ct)
            kb_ref[pl.ds(r, ct), :] = kn_ref[pl.ds(r, ct), :].astype(BF16)
            vb_ref[pl.ds(r, ct), :] = vn_ref[pl.ds(r, ct), :].astype(BF16)
            return carry

        lax.fori_loop(0, t // ct, cast, 0)

    qb = (q_ref[...] * (scale * LOG2_E)).astype(BF16)
    u = u_ref[...]

    def tile(kt, masked=False):
        r = pl.multiple_of(kt * tq, tq)
        return kb_ref[pl.ds(r, tq), :], vb_ref[pl.ds(r, tq), :], masked

    zero = (jnp.zeros((tq, 1), F32), jnp.zeros((tq, dh), F32))

    def head_pair(_):
        return _sb_group(qb, [tile(i, True), tile(i - 1)], zero[0], zero[1], u) + (i - 2,)

    def head_alone(_):
        return _sb_group(qb, [tile(i, True)], zero[0], zero[1], u) + (i - 1,)

    run, acc, top = lax.cond(i >= 1, head_pair, head_alone, None)

    def alive(r):
        return jnp.max(r) > SB_DEAD_LOG2

    def rest(run, acc, top):
        left = top + 1
        size = 1
        while size <= SB_GROUP:
            full = size == SB_GROUP
            count = lax.div(left, size) if full else lax.rem(lax.div(left, size), 2)

            def group(c, size=size, top=top):
                first = top - c[0] * size
                r, a = _sb_group(qb, [tile(first - j) for j in range(size)], c[1], c[2], u)
                return c[0] + 1, r, a

            def more(c, count=count):
                return jnp.logical_and(c[0] < count, alive(c[1]))

            _, run, acc = lax.while_loop(more, group, (jnp.int32(0), run, acc))
            top = top - count * size
            size *= 2
        return acc

    acc = lax.cond(jnp.logical_and(top >= 0, alive(run)), rest, lambda run, acc, top: acc,
                   run, acc, top)
    o_ref[...] = acc.astype(o_ref.dtype)


def _stick_breaking(q, k_new, v_new, n_heads, dh):
    b, t, _ = q.shape
    tq = min(256, t)
    u = (lax.broadcasted_iota(jnp.int32, (tq, tq), 0)
         > lax.broadcasted_iota(jnp.int32, (tq, tq), 1)).astype(BF16)
    kern = functools.partial(_sb_kernel, tq=tq, scale=dh ** -0.5)
    return pl.pallas_call(
        kern,
        grid=(b, n_heads, t // tq),
        in_specs=[pl.BlockSpec((None, tq, dh), lambda bi, h, i: (bi, i, h)),
                  pl.BlockSpec((None, t, dh), lambda bi, h, i: (bi, 0, h)),
                  pl.BlockSpec((None, t, dh), lambda bi, h, i: (bi, 0, h)),
                  pl.BlockSpec((tq, tq), lambda bi, h, i: (0, 0))],
        out_specs=pl.BlockSpec((None, tq, dh), lambda bi, h, i: (bi, i, h)),
        out_shape=jax.ShapeDtypeStruct((b, t, n_heads * dh), BF16),
        scratch_shapes=[pltpu.VMEM((t, dh), BF16), pltpu.VMEM((t, dh), BF16)],
        compiler_params=_cparams("parallel", "parallel", "arbitrary"),
        name="stick_breaking",
    )(q, k_new, v_new, u)


def _sbd_kernel(q_ref, kn_ref, vn_ref, kc_hbm, vc_hbm, u_ref, qpos_ref, o_ref,
                qbd_ref, run_ref, acc_ref, kbuf_ref, vbuf_ref, sem, *,
                layer, n_ct, n_heads, dh, t, tk, hg, scale):
    bi = pl.program_id(0)
    rows = tk * n_heads

    def fetch(j):
        r = pl.multiple_of(j * rows, rows)
        return (pltpu.make_async_copy(kc_hbm.at[layer, bi, pl.ds(r, rows), :], kbuf_ref, sem.at[0]),
                pltpu.make_async_copy(vc_hbm.at[layer, bi, pl.ds(r, rows), :], vbuf_ref, sem.at[1]))

    for cp in fetch(n_ct - 1):
        cp.start()
    qbd_ref[...] = jnp.zeros_like(qbd_ref)
    for h in range(n_heads):
        qbd_ref[h * t:(h + 1) * t, h * dh:(h + 1) * dh] = (
            q_ref[:, h * dh:(h + 1) * dh] * (scale * LOG2_E)).astype(BF16)
    run_ref[...] = jnp.zeros_like(run_ref)
    acc_ref[...] = jnp.zeros_like(acc_ref)

    def tile(get_k, get_v, masked):
        zt = None
        for g in range(n_heads // hg):
            kcat = [get_k(g * hg + j) for j in range(hg)]
            kcat = (jnp.concatenate(kcat, axis=1) if hg > 1 else kcat[0]).astype(BF16)
            part = lax.dot_general(kcat, qbd_ref[:, g * hg * dh:(g + 1) * hg * dh],
                                   (((1,), (1,)), ((), ())), preferred_element_type=F32)
            zt = part if zt is None else zt + part
        log_beta, log_1m = _sb_log2_terms(zt)
        vis = None
        if masked:
            vis = lax.broadcasted_iota(jnp.int32, zt.shape, 0) < qpos_ref[...]
            log_1m = jnp.where(vis, log_1m, 0.0)
        later = _sb_split_dot(log_1m, u_ref[...], True)
        w = jnp.exp2(log_beta + (later + run_ref[...]))
        if masked:
            w = jnp.where(vis, w, 0.0)
        run_ref[...] += later[0:1, :] + log_1m[0:1, :]
        wt = w.T
        for h in range(n_heads):
            acc_ref[:, h * dh:(h + 1) * dh] += jnp.dot(
                wt[h * t:(h + 1) * t, :].astype(BF16), get_v(h).astype(BF16),
                preferred_element_type=F32)
        return (jnp.max(run_ref[...]) > SB_DEAD_LOG2).astype(jnp.int32)

    pad = jnp.zeros((tk - t, dh), F32)
    alive = tile(lambda h: jnp.concatenate([kn_ref[:, h * dh:(h + 1) * dh], pad], axis=0),
                 lambda h: jnp.concatenate([vn_ref[:, h * dh:(h + 1) * dh], pad], axis=0), True)
    for cp in fetch(n_ct - 1):
        cp.wait()

    def cache_tile(c):
        n = c[0]
        still = tile(lambda h: kbuf_ref[pl.ds(h, tk, stride=n_heads), :],
                     lambda h: vbuf_ref[pl.ds(h, tk, stride=n_heads), :], False)

        @pl.when(jnp.logical_and(still > 0, n + 1 < n_ct))
        def _():
            for cp in fetch(n_ct - 2 - n):
                cp.start()
            for cp in fetch(n_ct - 2 - n):
                cp.wait()

        return n + 1, still

    lax.while_loop(lambda c: jnp.logical_and(c[0] < n_ct, c[1] > 0), cache_tile, (jnp.int32(0), alive))
    o_ref[...] = acc_ref[...].astype(o_ref.dtype)


def _stick_breaking_decode(q, k_new, v_new, cache_k, cache_v, layer, n_heads, dh):
    b, t, hd = q.shape
    tp = cache_k.shape[2]
    tk = min(256, tp)
    n_ct = tp // tk
    nq = n_heads * t
    nqp = -(-nq // LANES) * LANES
    hg = max(1, 2 * LANES // dh)
    if n_heads % hg:
        hg = 1
    u = (lax.broadcasted_iota(jnp.int32, (tk, tk), 1)
         > lax.broadcasted_iota(jnp.int32, (tk, tk), 0)).astype(BF16)
    col = jnp.arange(nqp, dtype=jnp.int32)
    qpos = jnp.where(col < nq, col % t, 0).reshape(1, nqp)
    ck = cache_k.reshape(cache_k.shape[0], b, tp * n_heads, dh)
    cv = cache_v.reshape(cache_v.shape[0], b, tp * n_heads, dh)
    cache_spec = pl.BlockSpec(memory_space=pl.ANY)
    new_spec = pl.BlockSpec((None, t, hd), lambda bi: (bi, 0, 0))
    kern = functools.partial(_sbd_kernel, layer=layer, n_ct=n_ct, n_heads=n_heads, dh=dh, t=t, tk=tk,
                             hg=hg, scale=dh ** -0.5)
    return pl.pallas_call(
        kern,
        grid=(b,),
        in_specs=[new_spec, new_spec, new_spec, cache_spec, cache_spec,
                  pl.BlockSpec((tk, tk), lambda bi: (0, 0)),
                  pl.BlockSpec((1, nqp), lambda bi: (0, 0))],
        out_specs=new_spec,
        out_shape=jax.ShapeDtypeStruct((b, t, hd), BF16),
        scratch_shapes=[pltpu.VMEM((nqp, hd), BF16), pltpu.VMEM((1, nqp), F32),
                        pltpu.VMEM((t, hd), F32),
                        pltpu.VMEM((tk * n_heads, dh), F32), pltpu.VMEM((tk * n_heads, dh), F32),
                        pltpu.SemaphoreType.DMA((2,))],
        compiler_params=_cparams("arbitrary"),
        name="stick_breaking_decode",
    )(q, k_new, v_new, ck, cv, u, qpos)


def _merge_kernel(ya_ref, yb_ref, wa_ref, wb_ref, ga_ref, gb_ref, o_ref):
    a = jnp.dot(ya_ref[...], wa_ref[...], preferred_element_type=F32)
    bb = jnp.dot(yb_ref[...], wb_ref[...], preferred_element_type=F32)
    o_ref[...] = (jax.nn.sigmoid(ga_ref[...]) * a + jax.nn.sigmoid(gb_ref[...]) * bb).astype(o_ref.dtype)


def _merge(ya, yb, wa, wb, gates):
    n, da = ya.shape
    db = yb.shape[1]
    d = wa.shape[1]
    tm = min(1024, n)
    tn = min(512, d)
    nj = d // tn
    return pl.pallas_call(
        _merge_kernel,
        grid=(n // tm, nj),
        in_specs=[pl.BlockSpec((tm, da), lambda i, j: (i, 0)),
                  pl.BlockSpec((tm, db), lambda i, j: (i, 0)),
                  pl.BlockSpec((da, tn), lambda i, j: (0, j)),
                  pl.BlockSpec((db, tn), lambda i, j: (0, j)),
                  pl.BlockSpec((tm, tn), lambda i, j: (i, j)),
                  pl.BlockSpec((tm, tn), lambda i, j: (i, nj + j))],
        out_specs=pl.BlockSpec((tm, tn), lambda i, j: (i, j)),
        out_shape=jax.ShapeDtypeStruct((n, d), BF16),
        compiler_params=_cparams("parallel", "arbitrary"),
        name="merge",
    )(ya, yb, wa, wb, gates, gates)


def _post_kernel(m_ref, wo_ref, x_ref, g1_ref, sc_ref, sh_ref, lg_ref, lb_ref, wr_ref, br_ref,
                 x1_ref, h2_ref, lo_ref, *, alpha):
    mix = jnp.dot(m_ref[...], wo_ref[...], preferred_element_type=F32)
    x1 = _layernorm(alpha * x_ref[...] + g1_ref[...] * mix, lg_ref[...], lb_ref[...])
    h2 = x1 * (1.0 + sc_ref[...]) + sh_ref[...]
    x1_ref[...] = x1
    h2_ref[...] = h2
    lo_ref[...] = jnp.dot(h2.astype(BF16), wr_ref[...], preferred_element_type=F32) + br_ref[...]


def _post(merged, wo, x2, g1, sc2, sh2, ln_g, ln_b, wr, br, t_per_batch, alpha):
    n, d = x2.shape
    e = wr.shape[1]
    tm = min(256, n)
    g1_op, g1_spec = _mod_operand(g1, n, t_per_batch, tm)
    sc_op, sc_spec = _mod_operand(sc2, n, t_per_batch, tm)
    sh_op, sh_spec = _mod_operand(sh2, n, t_per_batch, tm)
    row = lambda i: (i, 0)
    fix = lambda i: (0, 0)
    return pl.pallas_call(
        functools.partial(_post_kernel, alpha=alpha),
        grid=(n // tm,),
        in_specs=[pl.BlockSpec((tm, d), row), pl.BlockSpec((d, d), fix), pl.BlockSpec((tm, d), row),
                  g1_spec, sc_spec, sh_spec,
                  pl.BlockSpec((1, d), fix), pl.BlockSpec((1, d), fix),
                  pl.BlockSpec((d, e), fix), pl.BlockSpec((1, e), fix)],
        out_specs=[pl.BlockSpec((tm, d), row), pl.BlockSpec((tm, d), row), pl.BlockSpec((tm, e), row)],
        out_shape=[jax.ShapeDtypeStruct((n, d), F32), jax.ShapeDtypeStruct((n, d), F32),
                   jax.ShapeDtypeStruct((n, e), F32)],
        compiler_params=_cparams("parallel"),
        name="post",
    )(merged, wo, x2, g1_op, sc_op, sh_op, ln_g.reshape(1, d), ln_b.reshape(1, d), wr,
      br.reshape(1, e))


def _cols_to_lanes(cols):
    tm = cols[0].shape[0]
    lane = lax.broadcasted_iota(jnp.int32, (tm, len(cols)), 1)
    out = jnp.broadcast_to(cols[0], (tm, len(cols)))
    for kk in range(1, len(cols)):
        out = jnp.where(lane == kk, cols[kk], out)
    return out


def _route_kernel(lo_ref, idx_ref, w_ref, rank_ref, cnt_ref, carry_ref):
    @pl.when(pl.program_id(0) == 0)
    def _():
        carry_ref[...] = jnp.zeros_like(carry_ref)

    lg = lo_ref[...]
    tm, e = lg.shape
    lane = lax.broadcasted_iota(jnp.int32, (tm, e), 1).astype(F32)
    vals, idxs = [], []
    chosen = jnp.zeros((tm, e), F32)
    for _ in range(TOP_K):
        m = jnp.max(lg, axis=-1, keepdims=True)
        ik = jnp.min(jnp.where(lg == m, lane, float(e)), axis=-1, keepdims=True)
        sel = lane == ik
        vals.append(m)
        idxs.append(ik)
        chosen = jnp.where(sel, 1.0, chosen)
        lg = jnp.where(sel, -jnp.inf, lg)
    ex = [jnp.exp(v - vals[0]) for v in vals]
    tot = ex[0]
    for t_ in ex[1:]:
        tot = tot + t_
    before = (lax.broadcasted_iota(jnp.int32, (tm, tm), 1)
              < lax.broadcasted_iota(jnp.int32, (tm, tm), 0)).astype(BF16)
    pos = jnp.dot(before, chosen.astype(BF16), preferred_element_type=F32) + carry_ref[...]
    ranks = [jnp.sum(jnp.where(lane == ik, pos, 0.0), axis=-1, keepdims=True) for ik in idxs]
    idx_ref[...] = _cols_to_lanes(idxs).astype(jnp.int32)
    w_ref[...] = _cols_to_lanes([x / tot for x in ex])
    rank_ref[...] = _cols_to_lanes(ranks).astype(jnp.int32)
    carry_ref[...] = carry_ref[...] + jnp.sum(chosen, axis=0, keepdims=True)
    cnt_ref[...] = carry_ref[...].astype(jnp.int32)


def _route(logits):
    n, e = logits.shape
    tm = min(256, n)
    row = lambda i: (i, 0)
    return pl.pallas_call(
        _route_kernel,
        grid=(n // tm,),
        in_specs=[pl.BlockSpec((tm, e), row)],
        out_specs=[pl.BlockSpec((tm, TOP_K), row), pl.BlockSpec((tm, TOP_K), row),
                   pl.BlockSpec((tm, TOP_K), row), pl.BlockSpec((1, e), lambda i: (0, 0))],
        out_shape=[jax.ShapeDtypeStruct((n, TOP_K), jnp.int32), jax.ShapeDtypeStruct((n, TOP_K), F32),
                   jax.ShapeDtypeStruct((n, TOP_K), jnp.int32), jax.ShapeDtypeStruct((1, e), jnp.int32)],
        scratch_shapes=[pltpu.VMEM((1, e), F32)],
        compiler_params=_cparams("arbitrary"),
        name="route",
    )(logits)


def _dest_kernel(idx_ref, rank_ref, start_ref, o_ref):
    idx = idx_ref[...]
    tm = idx.shape[0]
    e = start_ref.shape[1]
    lane = lax.broadcasted_iota(jnp.int32, (tm, e), 1)
    start = start_ref[...].astype(F32)
    cols = [jnp.sum(jnp.where(lane == idx[:, kk:kk + 1], start, 0.0), axis=-1, keepdims=True)
            for kk in range(TOP_K)]
    o_ref[...] = _cols_to_lanes(cols).astype(jnp.int32) + rank_ref[...]


def _dest(idx, rank, group_start):
    n = idx.shape[0]
    e = group_start.shape[1]
    tm = min(256, n)
    row = lambda i: (i, 0)
    return pl.pallas_call(
        _dest_kernel,
        grid=(n // tm,),
        in_specs=[pl.BlockSpec((tm, TOP_K), row), pl.BlockSpec((tm, TOP_K), row),
                  pl.BlockSpec((1, e), lambda i: (0, 0))],
        out_specs=pl.BlockSpec((tm, TOP_K), row),
        out_shape=jax.ShapeDtypeStruct((n, TOP_K), jnp.int32),
        compiler_params=_cparams("parallel"),
        name="dest",
    )(idx, rank, group_start)


def _split_gate_up_kernel(w_ref, p_ref, g_ref, u_ref):
    x = w_ref[...].astype(BF16)
    p = p_ref[...]
    width = p.shape[0]
    half = width // 2
    for g in range(x.shape[1] // width):
        res = jnp.dot(x[:, g * width:(g + 1) * width], p, preferred_element_type=F32)
        g_ref[:, g * half:(g + 1) * half] = res[:, :half].astype(BF16)
        u_ref[:, g * half:(g + 1) * half] = res[:, half:].astype(BF16)


def _split_gate_up(w):
    e, d, f2 = w.shape
    width = 2 * LANES
    tr = min(1024, d)
    tc = min(1024, f2)
    r = lax.broadcasted_iota(jnp.int32, (width, width), 0)
    c = lax.broadcasted_iota(jnp.int32, (width, width), 1)
    perm = jnp.where(c < LANES, r == 2 * c, r == 2 * (c - LANES) + 1).astype(BF16)
    return pl.pallas_call(
        _split_gate_up_kernel,
        grid=(e, d // tr, f2 // tc),
        in_specs=[pl.BlockSpec((None, tr, tc), lambda ei, ri, ci: (ei, ri, ci)),
                  pl.BlockSpec((width, width), lambda ei, ri, ci: (0, 0))],
        out_specs=[pl.BlockSpec((None, tr, tc // 2), lambda ei, ri, ci: (ei, ri, ci)),
                   pl.BlockSpec((None, tr, tc // 2), lambda ei, ri, ci: (ei, ri, ci))],
        out_shape=[jax.ShapeDtypeStruct((e, d, f2 // 2), BF16),
                   jax.ShapeDtypeStruct((e, d, f2 // 2), BF16)],
        compiler_params=_cparams("parallel", "parallel", "parallel"),
        name="split_gate_up",
    )(w, perm)


def _row_copy(src_ref, src_row, dst_ref, dst_row, sem):
    return pltpu.make_async_copy(src_ref.at[pl.ds(src_row, 1), :], dst_ref.at[pl.ds(dst_row, 1), :], sem)


def _dispatch_kernel(dest_ref, h_ref, xs_in_ref, xs_ref, sem):
    del xs_in_ref
    tm = h_ref.shape[0]

    def issue(g, c):
        base = pl.multiple_of(g * SUBLANES, SUBLANES)
        for r in range(SUBLANES):
            for kk in range(TOP_K):
                _row_copy(h_ref, base + r, xs_ref, dest_ref[(base + r) * TOP_K + kk], sem).start(
                    priority=kk % 2)
        return c

    lax.fori_loop(0, tm // SUBLANES, issue, 0)

    def drain(t, c):
        for kk in range(TOP_K):
            _row_copy(h_ref, 0, xs_ref, 0, sem).wait()
        return c

    lax.fori_loop(0, tm, drain, 0)


def _dispatch(dest_flat, h2, n_rows):
    n, d = h2.shape
    tm = min(256, n)
    xs0 = jnp.zeros((n_rows, d), F32)
    return pl.pallas_call(
        _dispatch_kernel,
        grid=(n // tm,),
        in_specs=[pl.BlockSpec((tm * TOP_K,), lambda i: (i,), memory_space=pltpu.SMEM),
                  pl.BlockSpec((tm, d), lambda i: (i, 0)),
                  pl.BlockSpec(memory_space=pl.ANY)],
        out_specs=pl.BlockSpec(memory_space=pl.ANY),
        out_shape=jax.ShapeDtypeStruct((n_rows, d), F32),
        scratch_shapes=[pltpu.SemaphoreType.DMA(())],
        input_output_aliases={2: 0},
        compiler_params=_cparams("arbitrary"),
        name="dispatch",
    )(dest_flat, h2, xs0)


EXPERT_ROW_SPLIT = 1


def _experts_kernel(be_ref, nu_ref, xs_ref, wg_ref, wu_ref, bg_ref, bu_ref, wd_ref, bd_ref,
                    y_ref, xb_ref):
    del be_ref
    j = pl.program_id(0)
    f = pl.program_id(1)

    @pl.when(j < nu_ref[0])
    def _():
        @pl.when(f == 0)
        def _():
            xb_ref[...] = xs_ref[...].astype(BF16)
            y_ref[...] = jnp.broadcast_to(bd_ref[...], y_ref.shape)

        rows = xb_ref.shape[0] // EXPERT_ROW_SPLIT
        for r in range(EXPERT_ROW_SPLIT):
            sl = slice(r * rows, (r + 1) * rows)
            xb = xb_ref[sl, :]
            gate = jnp.dot(xb, wg_ref[...], preferred_element_type=F32) + bg_ref[...]
            up = jnp.dot(xb, wu_ref[...], preferred_element_type=F32) + bu_ref[...]
            gate = jnp.minimum(gate, SWIGLU_LIMIT)
            up = jnp.clip(up, -SWIGLU_LIMIT, SWIGLU_LIMIT)
            act = (up + 1.0) * (gate * jax.nn.sigmoid(SWIGLU_ALPHA * gate))
            y_ref[sl, :] += jnp.dot(act.astype(BF16), wd_ref[...], preferred_element_type=F32)

    @pl.when(jnp.logical_and(j >= nu_ref[0], f == 0))
    def _():
        y_ref[...] = jnp.zeros_like(y_ref)


def _experts(xs, block_e, n_used, wg, wu, bg, bu, wd, bd, rb):
    n_rows, d = xs.shape
    e, _, dff = wg.shape
    nb = n_rows // rb
    tf = _tile(dff, 1024)
    nf = dff // tf

    def live(j, nu):
        return jnp.minimum(j, nu[0] - 1)

    def fidx(j, f, nu):
        return jnp.where(j < nu[0], f, nf - 1)

    grid_spec = pltpu.PrefetchScalarGridSpec(
        num_scalar_prefetch=2,
        grid=(nb, nf),
        in_specs=[
            pl.BlockSpec((rb, d), lambda j, f, be, nu: (live(j, nu), 0)),
            pl.BlockSpec((None, d, tf), lambda j, f, be, nu: (be[live(j, nu)], 0, fidx(j, f, nu))),
            pl.BlockSpec((None, d, tf), lambda j, f, be, nu: (be[live(j, nu)], 0, fidx(j, f, nu))),
            pl.BlockSpec((None, 1, tf), lambda j, f, be, nu: (be[live(j, nu)], 0, fidx(j, f, nu))),
            pl.BlockSpec((None, 1, tf), lambda j, f, be, nu: (be[live(j, nu)], 0, fidx(j, f, nu))),
            pl.BlockSpec((None, tf, d), lambda j, f, be, nu: (be[live(j, nu)], fidx(j, f, nu), 0)),
            pl.BlockSpec((None, 1, d), lambda j, f, be, nu: (be[live(j, nu)], 0, 0)),
        ],
        out_specs=pl.BlockSpec((rb, d), lambda j, f, be, nu: (j, 0)),
        scratch_shapes=[pltpu.VMEM((rb, d), BF16)],
    )
    return pl.pallas_call(
        _experts_kernel,
        grid_spec=grid_spec,
        out_shape=jax.ShapeDtypeStruct((n_rows, d), F32),
        compiler_params=_cparams("arbitrary", "arbitrary"),
        name="experts",
    )(block_e, n_used, xs, wg, wu, bg, bu, wd, bd)


def _combine_kernel(dest_ref, y_ref, w_ref, x1_ref, g2_ref, lg_ref, lb_ref, o_ref, buf_ref, sem,
                    *, alpha):
    tm = x1_ref.shape[0]

    def issue(g, c):
        base = pl.multiple_of(g * SUBLANES, SUBLANES)
        for r in range(SUBLANES):
            for kk in range(TOP_K):
                _row_copy(y_ref, dest_ref[(base + r) * TOP_K + kk], buf_ref.at[kk], base + r, sem).start()
        return c

    lax.fori_loop(0, tm // SUBLANES, issue, 0)

    def drain(t, c):
        for kk in range(TOP_K):
            _row_copy(y_ref, 0, buf_ref.at[kk], 0, sem).wait()
        return c

    lax.fori_loop(0, tm, drain, 0)
    w = w_ref[...]
    ffn = w[:, 0:1] * buf_ref[0]
    for kk in range(1, TOP_K):
        ffn = ffn + w[:, kk:kk + 1] * buf_ref[kk]
    o_ref[...] = _layernorm(alpha * x1_ref[...] + g2_ref[...] * ffn, lg_ref[...], lb_ref[...])


def _combine(dest_flat, y, top_w, x1, g2, ln_g, ln_b, t_per_batch, alpha):
    n, d = x1.shape
    tm = min(256, n)
    g2_op, g2_spec = _mod_operand(g2, n, t_per_batch, tm)
    row = lambda i: (i, 0)
    fix = lambda i: (0, 0)
    return pl.pallas_call(
        functools.partial(_combine_kernel, alpha=alpha),
        grid=(n // tm,),
        in_specs=[pl.BlockSpec((tm * TOP_K,), lambda i: (i,), memory_space=pltpu.SMEM),
                  pl.BlockSpec(memory_space=pl.ANY),
                  pl.BlockSpec((tm, TOP_K), row), pl.BlockSpec((tm, d), row), g2_spec,
                  pl.BlockSpec((1, d), fix), pl.BlockSpec((1, d), fix)],
        out_specs=pl.BlockSpec((tm, d), row),
        out_shape=jax.ShapeDtypeStruct((n, d), F32),
        scratch_shapes=[pltpu.VMEM((TOP_K, tm, d), F32), pltpu.SemaphoreType.DMA(())],
        compiler_params=_cparams("arbitrary"),
        name="combine",
    )(dest_flat, y, top_w, x1, g2_op, ln_g.reshape(1, d), ln_b.reshape(1, d))


def _moe(h2, logits, x1, g2, ln_g, ln_b, ew, t_per_batch, alpha, rb):
    n, d = h2.shape
    e = logits.shape[1]
    idx, top_w, rank, counts = _route(logits)
    padded = (counts + rb - 1) // rb * rb
    group_end = jnp.cumsum(padded, axis=1)
    dest = _dest(idx, rank, group_end - padded)
    dest_flat = dest.reshape(n * TOP_K)
    nb = -(-(n * TOP_K) // rb) + e
    block_e = jnp.minimum(jnp.searchsorted(group_end[0], jnp.arange(nb, dtype=jnp.int32) * rb,
                                           side='right'), e - 1).astype(jnp.int32)
    n_used = (group_end[0, -1:] // rb).astype(jnp.int32)
    xs = _dispatch(dest_flat, h2, nb * rb)
    y = _experts(xs, block_e, n_used, *ew, rb=rb)
    return _combine(dest_flat, y, top_w, x1, g2, ln_g, ln_b, t_per_batch, alpha)


def _layer(x, mod, s0, cache, p, alpha):
    b, t, d = x.shape
    n = b * t
    sh1, sc1, g1, sh2, sc2, g2 = [mod[:, i * d:(i + 1) * d] for i in range(6)]
    x2 = x.reshape(n, d)
    hg, dk, dv, hs, dh = p['dims']
    g4 = _modmm(x2, sc1, sh1, p['w_g4'], t, name="proj_gla")
    la = _loga(x2, sc1, sh1, p['w_gf'], p['w_gla_f'], p['b_gla_f'], t)
    sq = _modmm(x2, sc1, sh1, p['w_sq'], t, name="proj_sq")
    sk = _modmm(x2, sc1, sh1, p['w_sk'], t, name="proj_sk")
    sv = _modmm(x2, sc1, sh1, p['w_sv'], t, name="proj_sv")
    gates = _modmm(x2, sc1, sh1, p['w_gates'], t, name="proj_gates")
    y_gla, s_fin = _gla(g4.reshape(b, t, -1), la.reshape(b, t, -1), p['gla_norm_g'], s0, hg, dk, dv)
    q3, k3, v3 = sq.reshape(b, t, -1), sk.reshape(b, t, -1), sv.reshape(b, t, -1)
    if cache is None:
        o_sb = _stick_breaking(q3, k3, v3, hs, dh)
    else:
        o_sb = _stick_breaking_decode(q3, k3, v3, cache[0], cache[1], cache[2], hs, dh)
    merged = _merge(y_gla.reshape(n, -1), o_sb.reshape(n, -1), p['w_gla_out'], p['w_sb_out'], gates)
    x1, h2, logits = _post(merged, p['w_o'], x2, g1, sc2, sh2, p['ln1_g'], p['ln1_b'],
                           p['w_router'], p['b_router'], t, alpha)
    rb = 512 if n * TOP_K >= 512 * p['n_experts'] * 4 else 128
    rb = min(rb, n)
    x_out = _moe(h2, logits, x1, g2, p['ln2_g'], p['ln2_b'], p['experts'], t, alpha, rb)
    return (x_out.reshape(b, t, d), s_fin, sk.reshape(b, t, hs, dh), sv.reshape(b, t, hs, dh))


def kernel(x_prompt, x_sample, c_prompt, c_sample, state_gla, cache_sb_k, cache_sb_v, w_ada, b_ada, w_in, w_gla_f, b_gla_f, gla_norm_g, w_gla_out, w_sb_out, w_o, ln1_g, ln1_b, w_router, b_router, w_gate_up, b_gate_up, w_down, b_down, ln2_g, ln2_b):
    depth = w_ada.shape[0]
    d = x_prompt.shape[-1]
    hg, dk, dv = state_gla.shape[2:]
    hs, dh = cache_sb_k.shape[3:]
    rank = w_gla_f.shape[1]
    n_experts = w_router.shape[-1]
    alpha = (2.0 * depth) ** 0.25
    qk, gv, sbw = hg * dk, hg * dv, hs * dh
    o_gf = 2 * qk + 2 * gv
    o_sq = o_gf + rank
    o_sk, o_sv, o_ga = o_sq + sbw, o_sq + 2 * sbw, o_sq + 3 * sbw
    rpad = -(-rank // LANES) * LANES
    n_p = x_prompt.shape[0]
    y_p, y_s = x_prompt, x_sample
    outs = [[] for _ in range(6)]
    for l in range(depth):
        wi = w_in[l]
        p = {
            'dims': (hg, dk, dv, hs, dh), 'n_experts': n_experts,
            'w_g4': wi[:, :o_gf].astype(BF16),
            'w_gf': jnp.pad(wi[:, o_gf:o_sq], ((0, 0), (0, rpad - rank))).astype(BF16),
            'w_gla_f': jnp.pad(w_gla_f[l], ((0, rpad - rank), (0, 0))).astype(BF16),
            'b_gla_f': b_gla_f[l],
            'w_sq': wi[:, o_sq:o_sk].astype(BF16), 'w_sk': wi[:, o_sk:o_sv].astype(BF16),
            'w_sv': wi[:, o_sv:o_ga].astype(BF16), 'w_gates': wi[:, o_ga:].astype(BF16),
            'gla_norm_g': gla_norm_g[l],
            'w_gla_out': w_gla_out[l].astype(BF16), 'w_sb_out': w_sb_out[l].astype(BF16),
            'w_o': w_o[l].astype(BF16), 'ln1_g': ln1_g[l], 'ln1_b': ln1_b[l],
            'w_router': w_router[l].astype(BF16), 'b_router': b_router[l],
            'ln2_g': ln2_g[l], 'ln2_b': ln2_b[l],
            'experts': tuple(_split_gate_up(w_gate_up[l])) + (
                b_gate_up[l][:, None, 0::2], b_gate_up[l][:, None, 1::2],
                w_down[l].astype(BF16), b_down[l][:, None, :]),
        }
        mod = _ada(jnp.concatenate([c_prompt, c_sample], axis=0), w_ada[l], b_ada[l])
        y_s, sf, kn, vn = _layer(y_s, mod[n_p:], state_gla[l], (cache_sb_k, cache_sb_v, l), p, alpha)
        outs[3].append(sf), outs[4].append(kn), outs[5].append(vn)
        y_p, sf, kn, vn = _layer(y_p, mod[:n_p], None, None, p, alpha)
        outs[0].append(sf), outs[1].append(kn), outs[2].append(vn)
    return (y_p, y_s) + tuple(jnp.stack(o) for o in outs)
```
